```python
import math
import jax, jax.numpy as jnp
from jax import lax
import numpy as np

D_MODEL = 1024
BATCH = 8
SEQ = 4096
DEPTH = 1

GRID_W = 64
HEAD_DIM = 64
NA_HEADS = 8
NA_WIDTH = NA_HEADS * HEAD_DIM
NA_KH_MAX = 8
NA_KW = 16
NA_COL_BLOCK = 16
NA_KEY_COLS = 32
DIFF_HEADS = 4
DIFF_WIDTH = DIFF_HEADS * 2 * HEAD_DIM
MIX_WIDTH = NA_WIDTH + DIFF_WIDTH
IN_WIDTH = 3 * NA_WIDTH + 3 * DIFF_WIDTH
N_EXPERTS = 16
EC_CAPACITY_FACTOR = 2
D_FF_EXPERT = 1024
PLE_DIM = 256
Q_BLOCK = 128
EPS = 1e-6

kernel_name = "hybrid_na_diffattn_ec_moe_encoder"


def rmsnorm(x, g):
    xf = x.astype(jnp.float32)
    y = xf * lax.rsqrt(jnp.mean(xf * xf, axis=-1, keepdims=True) + EPS)
    return (y * g.astype(jnp.float32)).astype(x.dtype)


def alibi_slopes(n_heads):
    return jnp.exp2(-8.0 * jnp.arange(1, n_heads + 1, dtype=jnp.float32) / n_heads)


def neighbourhood_attention(q, k, v, rpb):
    B, S, H, dh = q.shape
    rows = S // GRID_W
    kh = min(NA_KH_MAX, rows)
    ncb = GRID_W // NA_COL_BLOCK
    qg = q.reshape(B, rows, ncb, NA_COL_BLOCK, H, dh)
    kg = k.reshape(B, rows, GRID_W, H, dh)
    vg = v.reshape(B, rows, GRID_W, H, dh)
    qcol = np.arange(GRID_W).reshape(ncb, NA_COL_BLOCK)
    kblk = np.clip(np.arange(ncb) * NA_COL_BLOCK - NA_KW // 2, 0, GRID_W - NA_KEY_COLS)
    kcol = kblk[:, None] + np.arange(NA_KEY_COLS)
    cstart = np.clip(qcol - NA_KW // 2, 0, GRID_W - NA_KW)
    kc = kcol[:, None, :]
    col_ok = (kc >= cstart[..., None]) & (kc < cstart[..., None] + NA_KW)
    dc = np.clip(kc - qcol[..., None] + NA_KW - 1, 0, 2 * NA_KW - 2)
    rpb_c = rpb[:, :, dc].astype(jnp.float32)
    mask = jnp.asarray(col_ok)[:, :, None, :]

    def row(r):
        rs = jnp.clip(r - kh // 2, 0, rows - kh)
        q_r = lax.dynamic_index_in_dim(qg, r, axis=1, keepdims=False)
        k_rows = lax.dynamic_slice_in_dim(kg, rs, kh, axis=1)
        v_rows = lax.dynamic_slice_in_dim(vg, rs, kh, axis=1)
        k_b = k_rows[:, :, kcol]
        v_b = v_rows[:, :, kcol]
        dr = rs + jnp.arange(kh) - r + NA_KH_MAX - 1
        bias = rpb_c[:, dr].transpose(0, 2, 3, 1, 4)
        s = jnp.einsum('bnqhd,brnjhd->bhnqrj', q_r, k_b).astype(jnp.float32)
        s = jnp.where(mask, s + bias[None], -jnp.inf)
        pr = jax.nn.softmax(s, axis=(-2, -1))
        o = jnp.einsum('bhnqrj,brnjhd->bnqhd', pr.astype(v.dtype), v_b)
        return o.reshape(B, GRID_W, H, dh)

    o = lax.map(row, jnp.arange(rows))
    return o.transpose(1, 0, 2, 3, 4).reshape(B, S, H, dh)


def differential_attention(q, k, v, lam, slopes):
    B, S, H, _, dh = q.shape
    nb = S // Q_BLOCK
    qb = q.reshape(B, nb, Q_BLOCK, H, 2, dh).transpose(1, 0, 2, 3, 4, 5)
    kpos = jnp.arange(S, dtype=jnp.float32)

    def block(args):
        q_i, i = args
        s = jnp.einsum('bqhcd,bkhcd->bhcqk', q_i, k).astype(jnp.float32)
        qpos = (i * Q_BLOCK + jnp.arange(Q_BLOCK)).astype(jnp.float32)
        dist = jnp.abs(qpos[:, None] - kpos[None, :])
        s = s - slopes[:, None, None, None] * dist
        pr = jax.nn.softmax(s, axis=-1)
        a = pr[:, :, 0] - lam * pr[:, :, 1]
        return jnp.einsum('bhqk,bkhe->bqhe', a.astype(v.dtype), v)

    o = lax.map(block, (qb, jnp.arange(nb)))
    return o.transpose(1, 0, 2, 3, 4).reshape(B, S, H, 2 * dh)


def expert_choice_moe(x, w_router, w_gate, w_up, w_down):
    B, S, D = x.shape
    cap = EC_CAPACITY_FACTOR * S // N_EXPERTS
    aff = jax.nn.softmax((x @ w_router).astype(jnp.float32), axis=-1)
    g, idx = lax.top_k(aff.transpose(0, 2, 1), cap)
    xs = jax.vmap(lambda xb, ib: xb[ib])(x, idx)
    hid = jax.nn.silu(jnp.einsum('becd,edf->becf', xs, w_gate)) * jnp.einsum('becd,edf->becf', xs, w_up)
    y = jnp.einsum('becf,efd->becd', hid, w_down) * g[..., None].astype(x.dtype)
    return jax.vmap(lambda yb, ib: jax.ops.segment_sum(
        yb.reshape(-1, D), ib.reshape(-1), num_segments=S))(y, idx)


def setup_inputs(seed: int = 0) -> dict:
    key = jax.random.key(seed)
    ks = jax.random.split(key, 24)
    f32 = jnp.float32
    nrm = lambda k, shape, scale: jax.random.normal(k, shape, f32) * scale
    gain = lambda k, shape: 1.0 + 0.05 * jax.random.normal(k, shape, f32)
    L, D, E, F = DEPTH, D_MODEL, N_EXPERTS, D_FF_EXPERT
    return {
        "x": nrm(ks[0], (BATCH, SEQ, D), 1.0),
        "p": nrm(ks[1], (L, BATCH, SEQ, PLE_DIM), 1.0),
        "g_mix": gain(ks[2], (L, D)),
        "w_in": nrm(ks[3], (L, D, IN_WIDTH), D ** -0.5),
        "na_rpb": nrm(ks[4], (L, NA_HEADS, 2 * NA_KH_MAX - 1, 2 * NA_KW - 1), 0.02),
        "g_na_out": gain(ks[5], (L, NA_HEADS, HEAD_DIM)),
        "lam_q1": nrm(ks[6], (L, HEAD_DIM), 0.1),
        "lam_k1": nrm(ks[7], (L, HEAD_DIM), 0.1),
        "lam_q2": nrm(ks[8], (L, HEAD_DIM), 0.1),
        "lam_k2": nrm(ks[9], (L, HEAD_DIM), 0.1),
        "g_diff_out": gain(ks[10], (L, 2 * HEAD_DIM)),
        "w_out": nrm(ks[11], (L, MIX_WIDTH, D), MIX_WIDTH ** -0.5),
        "g_moe": gain(ks[12], (L, D)),
        "w_router": nrm(ks[13], (L, D, E), D ** -0.5),
        "w_gate": nrm(ks[14], (L, E, D, F), D ** -0.5),
        "w_up": nrm(ks[15], (L, E, D, F), D ** -0.5),
        "w_down": nrm(ks[16], (L, E, F, D), F ** -0.5),
        "g_ple": gain(ks[17], (L, D)),
        "w_ple_gate": nrm(ks[18], (L, D, D), D ** -0.5),
        "w_ple_proj": nrm(ks[19], (L, PLE_DIM, D), PLE_DIM ** -0.5),
        "g_final": gain(ks[20], (D,)),
    }


def reference(x, p, g_mix, w_in, na_rpb, g_na_out, lam_q1, lam_k1, lam_q2, lam_k2,
              g_diff_out, w_out, g_moe, w_router, w_gate, w_up, w_down,
              g_ple, w_ple_gate, w_ple_proj, g_final):
    B, S, D = x.shape
    scale = HEAD_DIM ** -0.5
    slopes = alibi_slopes(DIFF_HEADS)
    splits = [NA_WIDTH, 2 * NA_WIDTH, 3 * NA_WIDTH,
              3 * NA_WIDTH + DIFF_WIDTH, 3 * NA_WIDTH + 2 * DIFF_WIDTH]
    h = x
    for i in range(DEPTH):
        hn = rmsnorm(h, g_mix[i])
        proj = hn @ w_in[i]
        q_na, k_na, v_na, q_df, k_df, v_df = jnp.split(proj, splits, axis=-1)

        o_na = neighbourhood_attention(
            q_na.reshape(B, S, NA_HEADS, HEAD_DIM) * scale,
            k_na.reshape(B, S, NA_HEADS, HEAD_DIM),
            v_na.reshape(B, S, NA_HEADS, HEAD_DIM),
            na_rpb[i])
        o_na = rmsnorm(o_na, g_na_out[i]).reshape(B, S, NA_WIDTH)

        lam_init = 0.8 - 0.6 * math.exp(-0.3 * i)
        lam = (jnp.exp(jnp.sum(lam_q1[i].astype(jnp.float32) * lam_k1[i].astype(jnp.float32)))
               - jnp.exp(jnp.sum(lam_q2[i].astype(jnp.float32) * lam_k2[i].astype(jnp.float32)))
               + lam_init)
        o_df = differential_attention(
            q_df.reshape(B, S, DIFF_HEADS, 2, HEAD_DIM) * scale,
            k_df.reshape(B, S, DIFF_HEADS, 2, HEAD_DIM),
            v_df.reshape(B, S, DIFF_HEADS, 2 * HEAD_DIM),
            lam, slopes)
        o_df = (rmsnorm(o_df, g_diff_out[i]) * (1.0 - lam_init)).reshape(B, S, DIFF_WIDTH)

        h = h + jnp.concatenate([o_na, o_df], axis=-1) @ w_out[i]

        h = h + expert_choice_moe(rmsnorm(h, g_moe[i]), w_router[i],
                                  w_gate[i], w_up[i], w_down[i])

        h = h + jax.nn.sigmoid(rmsnorm(h, g_ple[i]) @ w_ple_gate[i]) * (p[i] @ w_ple_proj[i])
    return rmsnorm(h, g_final)
```

```python
import functools
import math

import jax
import jax.numpy as jnp
import numpy as np
from jax import lax
from jax.experimental import pallas as pl
from jax.experimental.pallas import tpu as pltpu

F32 = jnp.float32
BF16 = jnp.bfloat16

EPS = 1e-6
GRID_W = 64
HEAD_DIM = 64
NA_HEADS = 8
NA_KH_MAX = 8
NA_KW = 16
DIFF_HEADS = 4
N_EXPERTS = 16
EC_CAPACITY_FACTOR = 2
LANES = 128
MOE_TILE = 256
MOE_WIN = 128
COMB_TILE = 128
NEG_BIG = -1e30
LOG2E = math.log2(math.e)
VMEM_LIMIT = 56 * 1024 * 1024


def _cparams(sem):
    return pltpu.CompilerParams(dimension_semantics=sem, vmem_limit_bytes=VMEM_LIMIT)


def _rms(x, g):
    return x * lax.rsqrt(jnp.mean(x * x, axis=-1, keepdims=True) + EPS) * g


def _inproj_kernel(x_ref, g_ref, w_ref, o_ref, *, n_chunk):
    xn = _rms(x_ref[...], g_ref[...]).astype(BF16)
    cw = o_ref.shape[-1] // n_chunk
    for j in range(n_chunk):
        o_ref[:, j * cw:(j + 1) * cw] = jnp.dot(
            xn, w_ref[:, j * cw:(j + 1) * cw], preferred_element_type=F32).astype(BF16)


def _inproj(x2, g, w, tm=512):
    n, d = x2.shape
    nout = w.shape[1]
    return pl.pallas_call(
        functools.partial(_inproj_kernel, n_chunk=nout // 512),
        grid=(n // tm,),
        in_specs=[pl.BlockSpec((tm, d), lambda i: (i, 0)),
                  pl.BlockSpec((1, d), lambda i: (0, 0)),
                  pl.BlockSpec((d, nout), lambda i: (0, 0))],
        out_specs=pl.BlockSpec((tm, nout), lambda i: (i, 0)),
        out_shape=jax.ShapeDtypeStruct((n, nout), BF16),
        compiler_params=_cparams(("parallel",)),
        name="inproj",
    )(x2, g, w)


def _na_kernel(q_ref, k_ref, v_ref, bias_ref, g_ref, o_ref, *, rows, kh):
    lane = lax.broadcasted_iota(jnp.int32, (GRID_W, LANES), 1)
    lo = lane < HEAD_DIM
    nkeys = kh * GRID_W
    zero = jnp.zeros((GRID_W, LANES), BF16)

    def row_body(r, carry):
        rs = jnp.clip(r - kh // 2, 0, rows - kh)
        off = rs - r + NA_KH_MAX - 1
        q0 = pl.multiple_of(r * GRID_W, GRID_W)
        k0 = pl.multiple_of(rs * GRID_W, GRID_W)
        for j in range(NA_HEADS // 2):
            cs = slice(j * LANES, (j + 1) * LANES)
            qj = q_ref[0, pl.ds(q0, GRID_W), cs]
            q2 = jnp.concatenate([jnp.where(lo, qj, zero), jnp.where(lo, zero, qj)], axis=0)
            kj = k_ref[0, pl.ds(k0, nkeys), cs]
            s = lax.dot_general(q2, kj, (((1,), (1,)), ((), ())), preferred_element_type=F32)
            bias_a = jnp.concatenate([bias_ref[2 * j, off + 2 * i] for i in range(kh // 2)], axis=1)
            bias_b = jnp.concatenate([bias_ref[2 * j + 1, off + 2 * i] for i in range(kh // 2)], axis=1)
            s = s + jnp.concatenate([bias_a, bias_b], axis=0)
            m = jnp.max(s, axis=-1, keepdims=True)
            p = jnp.exp(s - m)
            l = jnp.sum(p, axis=-1, keepdims=True)
            vj = v_ref[0, pl.ds(k0, nkeys), cs]
            of = jnp.dot(p.astype(BF16), vj, preferred_element_type=F32) / l
            o = jnp.where(lo, of[:GRID_W], of[GRID_W:])
            sq = o * o
            ms_a = jnp.sum(jnp.where(lo, sq, 0.0), axis=-1, keepdims=True) * (1.0 / HEAD_DIM)
            ms_b = jnp.sum(jnp.where(lo, 0.0, sq), axis=-1, keepdims=True) * (1.0 / HEAD_DIM)
            inv = jnp.where(lo, lax.rsqrt(ms_a + EPS), lax.rsqrt(ms_b + EPS))
            o_ref[0, pl.ds(q0, GRID_W), cs] = (o * inv * g_ref[:, cs]).astype(BF16)
        return carry

    lax.fori_loop(0, rows, row_body, 0)


def _na_attention(proj, bias, g):
    b, s, _ = proj.shape
    rows = s // GRID_W
    kh = min(NA_KH_MAX, rows)
    w = NA_HEADS * HEAD_DIM
    return pl.pallas_call(
        functools.partial(_na_kernel, rows=rows, kh=kh),
        grid=(b,),
        in_specs=[pl.BlockSpec((1, s, w), lambda i: (i, 0, 0)),
                  pl.BlockSpec((1, s, w), lambda i: (i, 0, 1)),
                  pl.BlockSpec((1, s, w), lambda i: (i, 0, 2)),
                  pl.BlockSpec(bias.shape, lambda i: (0, 0, 0, 0)),
                  pl.BlockSpec((1, w), lambda i: (0, 0))],
        out_specs=pl.BlockSpec((1, s, w), lambda i: (i, 0, 0)),
        out_shape=jax.ShapeDtypeStruct((b, s, w), BF16),
        compiler_params=_cparams(("parallel",)),
        name="na_attn",
    )(proj, proj, proj, bias, g)


def _na_bias_table(rpb):
    qcol = np.arange(GRID_W)[:, None]
    kcol = np.arange(GRID_W)[None, :]
    cstart = np.clip(qcol - NA_KW // 2, 0, GRID_W - NA_KW)
    ok = (kcol >= cstart) & (kcol < cstart + NA_KW)
    dc = np.clip(kcol - qcol + NA_KW - 1, 0, 2 * NA_KW - 2)
    t = jnp.where(jnp.asarray(ok)[None, None], rpb.astype(F32)[:, :, dc], NEG_BIG)
    return jnp.concatenate([t[:, :-1], t[:, 1:]], axis=-1)


def _df_kernel(q_ref, k_ref, v_ref, lam_ref, g_ref, o_ref,
               bias_scr, s0_scr, s1_scr, p0_scr, p1_scr, a0_scr, a1_scr, m_scr, l_scr, acc_scr,
               *, tq, tk, rb, s_len, lam_init):
    h = pl.program_id(1)
    qi = pl.program_id(2)
    slope = jnp.exp2(-8.0 * (h + 1).astype(F32) / DIFF_HEADS) * LOG2E
    n_kb = s_len // tk

    @pl.when(qi == 0)
    def _():
        a = lax.broadcasted_iota(jnp.int32, (tq, tk), 0)
        b = lax.broadcasted_iota(jnp.int32, (tq, tk), 1)
        dist = (a - b).astype(F32)
        bias_scr[0] = slope * dist
        bias_scr[1] = -slope * dist
        for t in range(tk // tq):
            bias_scr[2 + t] = slope * jnp.abs(dist + float(t * tq))

    lane = lax.broadcasted_iota(jnp.int32, (tq, LANES), 1)
    lo = lane < HEAD_DIM
    q = q_ref[0]
    zero = jnp.zeros_like(q)
    q2 = jnp.concatenate([jnp.where(lo, q, zero), jnp.where(lo, zero, q)], axis=0)

    def scores(kb, s_scr):
        k0 = pl.multiple_of(kb * tk, tk)
        s_scr[...] = lax.dot_general(q2, k_ref[0, pl.ds(k0, tk), :], (((1,), (1,)), ((), ())),
                                     preferred_element_type=F32)

    def softmax(kb, s_scr, p_scr, a_scr):
        c = qi * tq - kb * tk
        before = c >= tk
        after = c <= -tq
        diag = jnp.right_shift(jnp.maximum(c, 0), int(math.log2(tq)))
        idx = jnp.where(before, 0, jnp.where(after, 1, 2 + diag))
        cf = slope * c.astype(F32)
        beta = jnp.where(before, -cf, jnp.where(after, cf, 0.0))
        for sb in range(2 * tq // rb):
            rows = slice(sb * rb, (sb + 1) * rb)
            brow = slice((sb * rb) % tq, (sb * rb) % tq + rb)
            s = s_scr[rows, :] - bias_scr[idx, brow, :]
            m_old = m_scr[rows, :]
            m_new = jnp.maximum(m_old, jnp.max(s, axis=-1, keepdims=True) + beta)
            alpha = jnp.exp2(m_old - m_new)
            shift = m_new - beta
            psum = None
            for j in range(tk // LANES):
                cols = slice(j * LANES, (j + 1) * LANES)
                pj = jnp.exp2(s[:, cols] - shift)
                p_scr[rows, cols] = pj.astype(BF16)
                psum = pj if psum is None else psum + pj
            l_scr[rows, :] = alpha * l_scr[rows, :] + psum
            m_scr[rows, :] = m_new
            a_scr[rows, :] = alpha

    def values(kb, p_scr, a_scr):
        k0 = pl.multiple_of(kb * tk, tk)
        acc_scr[...] = a_scr[...] * acc_scr[...] + jnp.dot(
            p_scr[...], v_ref[0, pl.ds(k0, tk), :], preferred_element_type=F32)

    m_scr[...] = jnp.full_like(m_scr, NEG_BIG)
    l_scr[...] = jnp.zeros_like(l_scr)
    acc_scr[...] = jnp.zeros_like(acc_scr)
    p1_scr[...] = jnp.zeros_like(p1_scr)
    a1_scr[...] = jnp.ones_like(a1_scr)
    scores(0, s0_scr)

    def pair_body(i, carry):
        kb = 2 * i
        scores(kb + 1, s1_scr)
        softmax(kb, s0_scr, p0_scr, a0_scr)
        values(jnp.maximum(kb - 1, 0), p1_scr, a1_scr)
        scores(jnp.minimum(kb + 2, n_kb - 1), s0_scr)
        softmax(kb + 1, s1_scr, p1_scr, a1_scr)
        values(kb, p0_scr, a0_scr)
        return carry

    lax.fori_loop(0, n_kb // 2, pair_body, 0)
    values(n_kb - 1, p1_scr, a1_scr)
    o2 = acc_scr[...] / jnp.sum(l_scr[...], axis=-1, keepdims=True)
    lv = lam_ref[...]
    lam = (jnp.exp(jnp.sum(lv[0:1] * lv[1:2], axis=-1, keepdims=True))
           - jnp.exp(jnp.sum(lv[2:3] * lv[3:4], axis=-1, keepdims=True)) + lam_init)
    o = o2[:tq] - lam * o2[tq:]
    o_ref[0] = (_rms(o, g_ref[...]) * (1.0 - lam_init)).astype(BF16)


def _df_attention(proj, lam_vecs, g, lam_init, tq=256, tk=512, rb=32):
    b, s, _ = proj.shape
    tq = min(tq, s)
    tk = min(tk, s)
    assert (s // tk) % 2 == 0 and tk % tq == 0
    qb, kb, vb = 12, 16, 20
    return pl.pallas_call(
        functools.partial(_df_kernel, tq=tq, tk=tk, rb=rb, s_len=s, lam_init=lam_init),
        grid=(b, DIFF_HEADS, s // tq),
        in_specs=[pl.BlockSpec((1, tq, LANES), lambda i, h, j: (i, j, qb + h)),
                  pl.BlockSpec((1, s, LANES), lambda i, h, j: (i, 0, kb + h)),
                  pl.BlockSpec((1, s, LANES), lambda i, h, j: (i, 0, vb + h)),
                  pl.BlockSpec((4, HEAD_DIM), lambda i, h, j: (0, 0)),
                  pl.BlockSpec((1, LANES), lambda i, h, j: (0, 0))],
        out_specs=pl.BlockSpec((1, tq, LANES), lambda i, h, j: (i, j, h)),
        out_shape=jax.ShapeDtypeStruct((b, s, DIFF_HEADS * LANES), BF16),
        scratch_shapes=[pltpu.VMEM((2 + tk // tq, tq, tk), F32),
                        pltpu.VMEM((2 * tq, tk), F32),
                        pltpu.VMEM((2 * tq, tk), F32),
                        pltpu.VMEM((2 * tq, tk), BF16),
                        pltpu.VMEM((2 * tq, tk), BF16),
                        pltpu.VMEM((2 * tq, LANES), F32),
                        pltpu.VMEM((2 * tq, LANES), F32),
                        pltpu.VMEM((2 * tq, LANES), F32),
                        pltpu.VMEM((2 * tq, LANES), F32),
                        pltpu.VMEM((2 * tq, LANES), F32)],
        compiler_params=_cparams(("parallel", "parallel", "arbitrary")),
        name="df_attn",
    )(proj, proj, proj, lam_vecs, g)


def _outproj_kernel(ona_ref, odf_ref, x_ref, wo_ref, g_ref, wr_ref, h1_ref, xn_ref, aff_ref):
    half = ona_ref.shape[-1]
    acc = jnp.dot(ona_ref[...], wo_ref[:half, :], preferred_element_type=F32)
    acc = acc + jnp.dot(odf_ref[...], wo_ref[half:, :], preferred_element_type=F32)
    h1 = x_ref[...] + acc
    h1_ref[...] = h1
    xn = _rms(h1, g_ref[...]).astype(BF16)
    xn_ref[...] = xn
    logits = jnp.dot(xn, wr_ref[...], preferred_element_type=F32)
    z = jnp.exp(logits - jnp.max(logits, axis=-1, keepdims=True))
    aff_ref[...] = z / jnp.sum(z, axis=-1, keepdims=True)


def _outproj(o_na, o_df, x2, wo, g, wr, tm=512):
    n, d = x2.shape
    half = o_na.shape[-1]
    e = wr.shape[-1]
    return pl.pallas_call(
        _outproj_kernel,
        grid=(n // tm,),
        in_specs=[pl.BlockSpec((tm, half), lambda i: (i, 0)),
                  pl.BlockSpec((tm, half), lambda i: (i, 0)),
                  pl.BlockSpec((tm, d), lambda i: (i, 0)),
                  pl.BlockSpec((2 * half, d), lambda i: (0, 0)),
                  pl.BlockSpec((1, d), lambda i: (0, 0)),
                  pl.BlockSpec((d, e), lambda i: (0, 0))],
        out_specs=[pl.BlockSpec((tm, d), lambda i: (i, 0)),
                   pl.BlockSpec((tm, d), lambda i: (i, 0)),
                   pl.BlockSpec((tm, e), lambda i: (i, 0))],
        out_shape=[jax.ShapeDtypeStruct((n, d), F32),
                   jax.ShapeDtypeStruct((n, d), BF16),
                   jax.ShapeDtypeStruct((n, e), F32)],
        compiler_params=_cparams(("parallel",)),
        name="outproj",
    )(o_na, o_df, x2, wo, g, wr)


def _excl_prefix(mask, upper):
    e, s = mask.shape
    mb = jnp.where(mask, 1.0, 0.0).astype(BF16)
    carry = jnp.zeros((e, 1), F32)
    outs = []
    for c in range(s // LANES):
        xc = mb[:, c * LANES:(c + 1) * LANES]
        outs.append(jnp.dot(xc, upper, preferred_element_type=F32) + carry)
        carry = carry + jnp.sum(xc.astype(F32), axis=-1, keepdims=True)
    return jnp.concatenate(outs, axis=1)


def _route_kernel(aff_ref, slot_ref, gate_ref, pref_ref, *, cap):
    a = aff_ref[0]
    e = a.shape[0]

    def count_ge(t):
        return jnp.sum(jnp.where(a >= t, 1, 0), axis=-1, keepdims=True)

    def bit_body(i, bits):
        trial = bits | jnp.left_shift(jnp.int32(1), 30 - i)
        return jnp.where(count_ge(pltpu.bitcast(trial, F32)) >= cap, trial, bits)

    thr = pltpu.bitcast(lax.fori_loop(0, 31, bit_body, jnp.zeros((e, 1), jnp.int32)), F32)
    step0 = thr
    for j in range(1, 21):
        trial = thr + step0 * (2.0 ** -(24 + j))
        thr = jnp.where(count_ge(trial) >= cap, trial, thr)
    gt = a > thr
    eq = a == thr
    need = (cap - jnp.sum(jnp.where(gt, 1, 0), axis=-1, keepdims=True)).astype(F32)
    r = lax.broadcasted_iota(jnp.int32, (LANES, LANES), 0)
    c = lax.broadcasted_iota(jnp.int32, (LANES, LANES), 1)
    upper = jnp.where(r < c, 1.0, 0.0).astype(BF16)
    sel = gt | (eq & (_excl_prefix(eq, upper) < need))
    pref = _excl_prefix(sel, upper)
    pref_ref[0] = pref
    slot_ref[0] = jnp.where(sel, pref, -1.0)
    gate_ref[0] = jnp.where(sel, a, 0.0)


def _route(aff_t, cap):
    b, e, s = aff_t.shape
    spec = pl.BlockSpec((1, e, s), lambda i: (i, 0, 0))
    shp = jax.ShapeDtypeStruct((b, e, s), F32)
    return pl.pallas_call(
        functools.partial(_route_kernel, cap=cap),
        grid=(b,),
        in_specs=[spec],
        out_specs=[spec, spec, spec],
        out_shape=[shp, shp, shp],
        compiler_params=_cparams(("parallel",)),
        name="route",
    )(aff_t)


def _window_range(ts_ref, base, t):
    s0 = ts_ref[base + t]
    s1 = ts_ref[base + t + 1]
    w_lo = jnp.right_shift(s0, int(math.log2(MOE_WIN)))
    w_hi = jnp.where(s1 > s0, jnp.right_shift(s1 - 1, int(math.log2(MOE_WIN))) + 1, w_lo)
    return w_lo, w_hi


def _moe_ffn_kernel(ts_ref, xn_ref, slot_ref, gate_ref, wg_ref, wu_ref, wd_ref, y_ref, xs_acc, gs_acc,
                    *, n_tile, n_fchunk):
    b = pl.program_id(0)
    e = pl.program_id(1)
    base = (b * pl.num_programs(1) + e) * (n_tile + 1)
    xs_acc[...] = jnp.zeros_like(xs_acc)
    gs_acc[...] = jnp.zeros_like(gs_acc)
    sub = lax.broadcasted_iota(jnp.int32, (MOE_WIN, MOE_TILE), 0).astype(F32)

    def tile_body(t, carry):
        w_lo, w_hi = _window_range(ts_ref, base, t)
        srow = slot_ref[0, 0, t]
        grow = gate_ref[0, 0, t]
        xt = xn_ref[0, pl.ds(pl.multiple_of(t * MOE_TILE, MOE_TILE), MOE_TILE), :]

        def win_body(w, c2):
            match = srow - (w * MOE_WIN).astype(F32) == sub
            onehot = jnp.where(match, 1.0, 0.0).astype(BF16)
            r0 = pl.multiple_of(w * MOE_WIN, MOE_WIN)
            xs_acc[pl.ds(r0, MOE_WIN), :] += jnp.dot(onehot, xt, preferred_element_type=F32)
            gs_acc[pl.ds(r0, MOE_WIN), :] += jnp.sum(jnp.where(match, grow, 0.0), axis=-1, keepdims=True)
            return c2

        lax.fori_loop(w_lo, w_hi, win_body, 0)
        return carry

    lax.fori_loop(0, n_tile, tile_body, 0)

    xs = xs_acc[...].astype(BF16)
    f = wg_ref.shape[-1]
    fc = f // n_fchunk
    y = None
    for j in range(n_fchunk):
        fs = slice(j * fc, (j + 1) * fc)
        gt = jnp.dot(xs, wg_ref[0, :, fs], preferred_element_type=F32)
        up = jnp.dot(xs, wu_ref[0, :, fs], preferred_element_type=F32)
        hid = (gt * jax.nn.sigmoid(gt) * up).astype(BF16)
        part = jnp.dot(hid, wd_ref[0, fs, :], preferred_element_type=F32)
        y = part if y is None else y + part
    y_ref[0, 0] = (y * gs_acc[...]).astype(BF16)


def _moe_ffn(tstart, xn, slot_rows, gate_rows, wg, wu, wd, cap):
    b, s, d = xn.shape
    e, _, f = wg.shape
    n_tile = s // MOE_TILE
    row_spec = pl.BlockSpec((1, 1, n_tile, 1, MOE_TILE), lambda i, j, ts: (i, j, 0, 0, 0))
    grid_spec = pltpu.PrefetchScalarGridSpec(
        num_scalar_prefetch=1,
        grid=(b, e),
        in_specs=[pl.BlockSpec((1, s, d), lambda i, j, ts: (i, 0, 0)),
                  row_spec, row_spec,
                  pl.BlockSpec((1, d, f), lambda i, j, ts: (j, 0, 0)),
                  pl.BlockSpec((1, d, f), lambda i, j, ts: (j, 0, 0)),
                  pl.BlockSpec((1, f, d), lambda i, j, ts: (j, 0, 0))],
        out_specs=pl.BlockSpec((1, 1, cap, d), lambda i, j, ts: (i, j, 0, 0)),
        scratch_shapes=[pltpu.VMEM((cap, d), F32), pltpu.VMEM((cap, 1), F32)],
    )
    return pl.pallas_call(
        functools.partial(_moe_ffn_kernel, n_tile=n_tile, n_fchunk=max(1, f // 256)),
        grid_spec=grid_spec,
        out_shape=jax.ShapeDtypeStruct((b, e, cap, d), BF16),
        compiler_params=_cparams(("parallel", "arbitrary")),
        name="moe_ffn",
    )(tstart, xn, slot_rows, gate_rows, wg, wu, wd)


def _moe_comb_kernel(ts_ref, y_ref, slot_ref, o_ref, *, n_tile, win):
    b = pl.program_id(0)
    t = pl.program_id(1)
    n_exp, cap = y_ref.shape[1], y_ref.shape[2]
    lane = lax.broadcasted_iota(jnp.int32, (COMB_TILE, win), 1).astype(F32)
    acc = None
    for e in range(n_exp):
        s0 = ts_ref[(b * n_exp + e) * (n_tile + 1) + t]
        start = jnp.minimum(jnp.left_shift(jnp.right_shift(s0, 4), 4), cap - win)
        start = pl.multiple_of(start, 16)
        scol = jnp.broadcast_to(slot_ref[0, 0, e], (LANES, COMB_TILE)).T - start.astype(F32)
        scol = jnp.tile(scol, (1, win // LANES))
        onehot = jnp.where(scol == lane, 1.0, 0.0).astype(BF16)
        part = jnp.dot(onehot, y_ref[0, e, pl.ds(start, win), :], preferred_element_type=F32)
        acc = part if acc is None else acc + part
    o_ref[0] = acc


def _moe_combine(tstart, y, slot_rows, s):
    b, e, cap, d = y.shape
    n_tile = s // COMB_TILE
    win = min(cap, 2 * COMB_TILE)
    assert win == cap or win >= COMB_TILE + 15
    grid_spec = pltpu.PrefetchScalarGridSpec(
        num_scalar_prefetch=1,
        grid=(b, n_tile),
        in_specs=[pl.BlockSpec((1, e, cap, d), lambda i, j, ts: (i, 0, 0, 0)),
                  pl.BlockSpec((1, 1, e, 1, COMB_TILE), lambda i, j, ts: (i, j, 0, 0, 0))],
        out_specs=pl.BlockSpec((1, COMB_TILE, d), lambda i, j, ts: (i, j, 0)),
    )
    return pl.pallas_call(
        functools.partial(_moe_comb_kernel, n_tile=n_tile, win=win),
        grid_spec=grid_spec,
        out_shape=jax.ShapeDtypeStruct((b, s, d), F32),
        compiler_params=_cparams(("parallel", "arbitrary")),
        name="moe_comb",
    )(tstart, y, slot_rows)


def _ple_kernel(h1_ref, moe_ref, p_ref, gp_ref, wg_ref, wp_ref, gf_ref, o_ref, *, last_layer):
    h2 = h1_ref[...] + moe_ref[...]
    hn = _rms(h2, gp_ref[...]).astype(BF16)
    gate = jax.nn.sigmoid(jnp.dot(hn, wg_ref[...], preferred_element_type=F32))
    proj = jnp.dot(p_ref[...].astype(BF16), wp_ref[...], preferred_element_type=F32)
    h3 = h2 + gate * proj
    o_ref[...] = _rms(h3, gf_ref[...]) if last_layer else h3


def _ple(h1, moe, p2, gp, wg, wp, gf, last_layer, tm=512):
    n, d = h1.shape
    dp = p2.shape[-1]
    row = pl.BlockSpec((tm, d), lambda i: (i, 0))
    vec = pl.BlockSpec((1, d), lambda i: (0, 0))
    return pl.pallas_call(
        functools.partial(_ple_kernel, last_layer=last_layer),
        grid=(n // tm,),
        in_specs=[row, row, pl.BlockSpec((tm, dp), lambda i: (i, 0)), vec,
                  pl.BlockSpec((d, d), lambda i: (0, 0)), pl.BlockSpec((dp, d), lambda i: (0, 0)), vec],
        out_specs=row,
        out_shape=jax.ShapeDtypeStruct((n, d), F32),
        compiler_params=_cparams(("parallel",)),
        name="ple",
    )(h1, moe, p2, gp, wg, wp, gf)


def kernel(x, p, g_mix, w_in, na_rpb, g_na_out, lam_q1, lam_k1, lam_q2, lam_k2, g_diff_out, w_out,
           g_moe, w_router, w_gate, w_up, w_down, g_ple, w_ple_gate, w_ple_proj, g_final):
    b, s, d = x.shape
    depth = w_in.shape[0]
    n = b * s
    na_w = NA_HEADS * HEAD_DIM
    cap = EC_CAPACITY_FACTOR * s // N_EXPERTS
    n_tile = s // MOE_TILE
    col_scale = np.ones((w_in.shape[-1],), np.float32)
    col_scale[:na_w] = HEAD_DIM ** -0.5
    col_scale[3 * na_w:3 * na_w + DIFF_HEADS * 2 * HEAD_DIM] = HEAD_DIM ** -0.5 * LOG2E

    h = x.reshape(n, d)
    for i in range(depth):
        lam_init = 0.8 - 0.6 * math.exp(-0.3 * i)
        proj = _inproj(h, g_mix[i][None], (w_in[i] * col_scale).astype(BF16)).reshape(b, s, -1)
        o_na = _na_attention(proj, _na_bias_table(na_rpb[i]), g_na_out[i].reshape(1, na_w))
        lam_vecs = jnp.stack([lam_q1[i], lam_k1[i], lam_q2[i], lam_k2[i]]).astype(F32)
        o_df = _df_attention(proj, lam_vecs, g_diff_out[i][None], lam_init)
        h1, xn, aff = _outproj(o_na.reshape(n, -1), o_df.reshape(n, -1), h, w_out[i].astype(BF16),
                               g_moe[i][None], w_router[i].astype(BF16))
        aff_t = aff.reshape(b, s, N_EXPERTS).transpose(0, 2, 1)
        slot, gate, pref = _route(aff_t, cap)
        last = jnp.full((b, N_EXPERTS, 1), cap, F32)
        tstart = jnp.concatenate([pref[:, :, ::MOE_TILE], last], axis=-1).astype(jnp.int32).reshape(-1)
        cstart = jnp.concatenate([pref[:, :, ::COMB_TILE], last], axis=-1).astype(jnp.int32).reshape(-1)
        slot_rows = slot.reshape(b, N_EXPERTS, n_tile, 1, MOE_TILE)
        gate_rows = gate.reshape(b, N_EXPERTS, n_tile, 1, MOE_TILE)
        y = _moe_ffn(tstart, xn.reshape(b, s, d), slot_rows, gate_rows, w_gate[i].astype(BF16),
                     w_up[i].astype(BF16), w_down[i].astype(BF16), cap)
        comb_rows = slot.reshape(b, N_EXPERTS, s // COMB_TILE, 1, COMB_TILE).transpose(0, 2, 1, 3, 4)
        moe = _moe_combine(cstart, y, comb_rows, s)
        h = _ple(h1, moe.reshape(n, d), p[i].reshape(n, -1), g_ple[i][None],
                 w_ple_gate[i].astype(BF16), w_ple_proj[i].astype(BF16), g_final[None],
                 last_layer=(i == depth - 1))
    return h.reshape(b, s, d)
```

```python
import functools
import math

import jax
import jax.numpy as jnp
import numpy as np
from jax import lax
from jax.experimental import pallas as pl
from jax.experimental.pallas import tpu as pltpu

F32 = jnp.float32
BF16 = jnp.bfloat16

EPS = 1e-6
GRID_W = 64
HEAD_DIM = 64
NA_HEADS = 8
NA_KH_MAX = 8
NA_KW = 16
DIFF_HEADS = 4
N_EXPERTS = 16
EC_CAPACITY_FACTOR = 2
LANES = 128
MOE_TILE = 256
MOE_WIN = 128
COMB_TILE = 128
NEG_BIG = -1e30
LOG2E = math.log2(math.e)
VMEM_LIMIT = 56 * 1024 * 1024


def _cparams(sem):
    return pltpu.CompilerParams(dimension_semantics=sem, vmem_limit_bytes=VMEM_LIMIT)


def _rms(x, g):
    return x * lax.rsqrt(jnp.mean(x * x, axis=-1, keepdims=True) + EPS) * g


def _inproj_kernel(x_ref, g_ref, w_ref, o_ref, *, n_chunk):
    xn = _rms(x_ref[...], g_ref[...]).astype(BF16)
    cw = o_ref.shape[-1] // n_chunk
    for j in range(n_chunk):
        o_ref[:, j * cw:(j + 1) * cw] = jnp.dot(
            xn, w_ref[:, j * cw:(j + 1) * cw], preferred_element_type=F32).astype(BF16)


def _inproj(x2, g, w, tm=512):
    n, d = x2.shape
    nout = w.shape[1]
    return pl.pallas_call(
        functools.partial(_inproj_kernel, n_chunk=nout // 512),
        grid=(n // tm,),
        in_specs=[pl.BlockSpec((tm, d), lambda i: (i, 0)),
                  pl.BlockSpec((1, d), lambda i: (0, 0)),
                  pl.BlockSpec((d, nout), lambda i: (0, 0))],
        out_specs=pl.BlockSpec((tm, nout), lambda i: (i, 0)),
        out_shape=jax.ShapeDtypeStruct((n, nout), BF16),
        compiler_params=_cparams(("parallel",)),
        name="inproj",
    )(x2, g, w)


def _na_kernel(q_ref, k_ref, v_ref, bias_ref, g_ref, o_ref,
               s0_scr, s1_scr, p0_scr, p1_scr, l0_scr, l1_scr, *, rows, kh, rb):
    lane = lax.broadcasted_iota(jnp.int32, (GRID_W, LANES), 1)
    lo = lane < HEAD_DIM
    nkeys = kh * GRID_W
    n_pair = NA_HEADS // 2
    zero = jnp.zeros((GRID_W, LANES), BF16)

    def geometry(r):
        rs = jnp.clip(r - kh // 2, 0, rows - kh)
        return (pl.multiple_of(r * GRID_W, GRID_W), pl.multiple_of(rs * GRID_W, GRID_W),
                rs - r + NA_KH_MAX - 1)

    def scores(r, s_scr):
        q0, k0, _ = geometry(r)
        for j in range(n_pair):
            cs = slice(j * LANES, (j + 1) * LANES)
            qj = q_ref[0, pl.ds(q0, GRID_W), cs]
            q2 = jnp.concatenate([jnp.where(lo, qj, zero), jnp.where(lo, zero, qj)], axis=0)
            s_scr[j] = lax.dot_general(q2, k_ref[0, pl.ds(k0, nkeys), cs], (((1,), (1,)), ((), ())),
                                       preferred_element_type=F32)

    def softmax(r, s_scr, p_scr, l_scr):
        _, _, off = geometry(r)
        for j in range(n_pair):
            for sb in range(2 * GRID_W // rb):
                rws = slice(sb * rb, (sb + 1) * rb)
                head = 2 * j + (sb * rb) // GRID_W
                brow = slice((sb * rb) % GRID_W, (sb * rb) % GRID_W + rb)
                chunks = [s_scr[j, rws, i * LANES:(i + 1) * LANES] + bias_ref[head, off + 2 * i, brow, :]
                          for i in range(kh // 2)]
                m = chunks[0]
                for c in chunks[1:]:
                    m = jnp.maximum(m, c)
                m = jnp.max(m, axis=-1, keepdims=True)
                lsum = None
                for i, c in enumerate(chunks):
                    pi = jnp.exp2(c - m)
                    p_scr[j, rws, i * LANES:(i + 1) * LANES] = pi.astype(BF16)
                    lsum = pi if lsum is None else lsum + pi
                l_scr[j, rws, :] = lsum

    def values(r, p_scr, l_scr):
        q0, k0, _ = geometry(r)
        for j in range(n_pair):
            cs = slice(j * LANES, (j + 1) * LANES)
            of = jnp.dot(p_scr[j], v_ref[0, pl.ds(k0, nkeys), cs], preferred_element_type=F32)
            of = of / jnp.sum(l_scr[j], axis=-1, keepdims=True)
            o = jnp.where(lo, of[:GRID_W], of[GRID_W:])
            sq = o * o
            ms_a = jnp.sum(jnp.where(lo, sq, 0.0), axis=-1, keepdims=True) * (1.0 / HEAD_DIM)
            ms_b = jnp.sum(jnp.where(lo, 0.0, sq), axis=-1, keepdims=True) * (1.0 / HEAD_DIM)
            inv = jnp.where(lo, lax.rsqrt(ms_a + EPS), lax.rsqrt(ms_b + EPS))
            o_ref[0, pl.ds(q0, GRID_W), cs] = (o * inv * g_ref[:, cs]).astype(BF16)

    p1_scr[...] = jnp.zeros_like(p1_scr)
    l1_scr[...] = jnp.ones_like(l1_scr)
    scores(0, s0_scr)

    def pair_body(i, carry):
        r = 2 * i
        scores(r + 1, s1_scr)
        softmax(r, s0_scr, p0_scr, l0_scr)
        values(jnp.maximum(r - 1, 0), p1_scr, l1_scr)
        scores(jnp.minimum(r + 2, rows - 1), s0_scr)
        softmax(r + 1, s1_scr, p1_scr, l1_scr)
        values(r, p0_scr, l0_scr)
        return carry

    lax.fori_loop(0, rows // 2, pair_body, 0)
    values(rows - 1, p1_scr, l1_scr)


def _na_attention(proj, bias, g, rb=32):
    b, s, _ = proj.shape
    rows = s // GRID_W
    kh = min(NA_KH_MAX, rows)
    assert rows % 2 == 0 and kh % 2 == 0 and GRID_W % rb == 0
    w = NA_HEADS * HEAD_DIM
    n_pair = NA_HEADS // 2
    stacked = 2 * GRID_W
    return pl.pallas_call(
        functools.partial(_na_kernel, rows=rows, kh=kh, rb=rb),
        grid=(b,),
        in_specs=[pl.BlockSpec((1, s, w), lambda i: (i, 0, 0)),
                  pl.BlockSpec((1, s, w), lambda i: (i, 0, 1)),
                  pl.BlockSpec((1, s, w), lambda i: (i, 0, 2)),
                  pl.BlockSpec(bias.shape, lambda i: (0, 0, 0, 0)),
                  pl.BlockSpec((1, w), lambda i: (0, 0))],
        out_specs=pl.BlockSpec((1, s, w), lambda i: (i, 0, 0)),
        out_shape=jax.ShapeDtypeStruct((b, s, w), BF16),
        scratch_shapes=[pltpu.VMEM((n_pair, stacked, kh * GRID_W), F32),
                        pltpu.VMEM((n_pair, stacked, kh * GRID_W), F32),
                        pltpu.VMEM((n_pair, stacked, kh * GRID_W), BF16),
                        pltpu.VMEM((n_pair, stacked, kh * GRID_W), BF16),
                        pltpu.VMEM((n_pair, stacked, LANES), F32),
                        pltpu.VMEM((n_pair, stacked, LANES), F32)],
        compiler_params=_cparams(("parallel",)),
        name="na_attn",
    )(proj, proj, proj, bias, g)


def _na_bias_table(rpb):
    qcol = np.arange(GRID_W)[:, None]
    kcol = np.arange(GRID_W)[None, :]
    cstart = np.clip(qcol - NA_KW // 2, 0, GRID_W - NA_KW)
    ok = (kcol >= cstart) & (kcol < cstart + NA_KW)
    dc = np.clip(kcol - qcol + NA_KW - 1, 0, 2 * NA_KW - 2)
    t = jnp.where(jnp.asarray(ok)[None, None], rpb.astype(F32)[:, :, dc] * LOG2E, NEG_BIG)
    return jnp.concatenate([t[:, :-1], t[:, 1:]], axis=-1)


def _df_kernel(q_ref, k_ref, v_ref, lam_ref, g_ref, o_ref,
               bias_scr, s0_scr, s1_scr, p0_scr, p1_scr, a0_scr, a1_scr, m_scr, l_scr, acc_scr,
               *, tq, tk, rb, s_len, lam_init):
    h = pl.program_id(1)
    qi = pl.program_id(2)
    slope = jnp.exp2(-8.0 * (h + 1).astype(F32) / DIFF_HEADS) * LOG2E
    n_kb = s_len // tk

    @pl.when(qi == 0)
    def _():
        a = lax.broadcasted_iota(jnp.int32, (tq, tk), 0)
        b = lax.broadcasted_iota(jnp.int32, (tq, tk), 1)
        dist = (a - b).astype(F32)
        bias_scr[0] = slope * dist
        bias_scr[1] = -slope * dist
        for t in range(tk // tq):
            bias_scr[2 + t] = slope * jnp.abs(dist + float(t * tq))

    lane = lax.broadcasted_iota(jnp.int32, (tq, LANES), 1)
    lo = lane < HEAD_DIM
    q = q_ref[0]
    zero = jnp.zeros_like(q)
    q2 = jnp.concatenate([jnp.where(lo, q, zero), jnp.where(lo, zero, q)], axis=0)

    def scores(kb, s_scr):
        s_scr[...] = lax.dot_general(q2, k_ref[0, kb * tk:(kb + 1) * tk, :], (((1,), (1,)), ((), ())),
                                     preferred_element_type=F32)

    def softmax(kb, s_scr, p_scr, a_scr):
        c = qi * tq - kb * tk
        before = c >= tk
        after = c <= -tq
        diag = jnp.right_shift(jnp.maximum(c, 0), int(math.log2(tq)))
        idx = jnp.where(before, 0, jnp.where(after, 1, 2 + diag))
        cf = slope * c.astype(F32)
        beta = jnp.where(before, -cf, jnp.where(after, cf, 0.0))
        for sb in range(2 * tq // rb):
            rows = slice(sb * rb, (sb + 1) * rb)
            brow = slice((sb * rb) % tq, (sb * rb) % tq + rb)
            s = s_scr[rows, :] - bias_scr[idx, brow, :]
            m_old = m_scr[rows, :]
            m_new = jnp.maximum(m_old, jnp.max(s, axis=-1, keepdims=True) + beta)
            alpha = jnp.exp2(m_old - m_new)
            shift = m_new - beta
            psum = None
            for j in range(tk // LANES):
                cols = slice(j * LANES, (j + 1) * LANES)
                pj = jnp.exp2(s[:, cols] - shift)
                p_scr[rows, cols] = pj.astype(BF16)
                psum = pj if psum is None else psum + pj
            l_scr[rows, :] = alpha * l_scr[rows, :] + psum
            m_scr[rows, :] = m_new
            a_scr[rows, :] = alpha

    def values(kb, p_scr, a_scr):
        acc_scr[...] = a_scr[...] * acc_scr[...] + jnp.dot(
            p_scr[...], v_ref[0, kb * tk:(kb + 1) * tk, :], preferred_element_type=F32)

    m_scr[...] = jnp.full_like(m_scr, NEG_BIG)
    l_scr[...] = jnp.zeros_like(l_scr)
    acc_scr[...] = jnp.zeros_like(acc_scr)
    s_bufs, p_bufs, a_bufs = (s0_scr, s1_scr), (p0_scr, p1_scr), (a0_scr, a1_scr)
    scores(0, s0_scr)
    for kb in range(n_kb):
        cur, oth = kb % 2, 1 - kb % 2
        if kb + 1 < n_kb:
            scores(kb + 1, s_bufs[oth])
        softmax(kb, s_bufs[cur], p_bufs[cur], a_bufs[cur])
        if kb >= 1:
            values(kb - 1, p_bufs[oth], a_bufs[oth])
    values(n_kb - 1, p_bufs[(n_kb - 1) % 2], a_bufs[(n_kb - 1) % 2])
    o2 = acc_scr[...] / jnp.sum(l_scr[...], axis=-1, keepdims=True)
    lv = lam_ref[...]
    lam = (jnp.exp(jnp.sum(lv[0:1] * lv[1:2], axis=-1, keepdims=True))
           - jnp.exp(jnp.sum(lv[2:3] * lv[3:4], axis=-1, keepdims=True)) + lam_init)
    o = o2[:tq] - lam * o2[tq:]
    o_ref[0] = (_rms(o, g_ref[...]) * (1.0 - lam_init)).astype(BF16)


def _df_attention(proj, lam_vecs, g, lam_init, tq=256, tk=512, rb=32):
    b, s, _ = proj.shape
    tq = min(tq, s)
    tk = min(tk, s)
    assert tk % tq == 0
    qb, kb, vb = 12, 16, 20
    return pl.pallas_call(
        functools.partial(_df_kernel, tq=tq, tk=tk, rb=rb, s_len=s, lam_init=lam_init),
        grid=(b, DIFF_HEADS, s // tq),
        in_specs=[pl.BlockSpec((1, tq, LANES), lambda i, h, j: (i, j, qb + h)),
                  pl.BlockSpec((1, s, LANES), lambda i, h, j: (i, 0, kb + h)),
                  pl.BlockSpec((1, s, LANES), lambda i, h, j: (i, 0, vb + h)),
                  pl.BlockSpec((4, HEAD_DIM), lambda i, h, j: (0, 0)),
                  pl.BlockSpec((1, LANES), lambda i, h, j: (0, 0))],
        out_specs=pl.BlockSpec((1, tq, LANES), lambda i, h, j: (i, j, h)),
        out_shape=jax.ShapeDtypeStruct((b, s, DIFF_HEADS * LANES), BF16),
        scratch_shapes=[pltpu.VMEM((2 + tk // tq, tq, tk), F32),
                        pltpu.VMEM((2 * tq, tk), F32),
                        pltpu.VMEM((2 * tq, tk), F32),
                        pltpu.VMEM((2 * tq, tk), BF16),
                        pltpu.VMEM((2 * tq, tk), BF16),
                        pltpu.VMEM((2 * tq, LANES), F32),
                        pltpu.VMEM((2 * tq, LANES), F32),
                        pltpu.VMEM((2 * tq, LANES), F32),
                        pltpu.VMEM((2 * tq, LANES), F32),
                        pltpu.VMEM((2 * tq, LANES), F32)],
        compiler_params=_cparams(("parallel", "parallel", "arbitrary")),
        name="df_attn",
    )(proj, proj, proj, lam_vecs, g)


def _outproj_kernel(ona_ref, odf_ref, x_ref, wo_ref, g_ref, wr_ref, h1_ref, xn_ref, aff_ref):
    half = ona_ref.shape[-1]
    acc = jnp.dot(ona_ref[...], wo_ref[:half, :], preferred_element_type=F32)
    acc = acc + jnp.dot(odf_ref[...], wo_ref[half:, :], preferred_element_type=F32)
    h1 = x_ref[...] + acc
    h1_ref[...] = h1
    xn = _rms(h1, g_ref[...]).astype(BF16)
    xn_ref[...] = xn
    logits = jnp.dot(xn, wr_ref[...], preferred_element_type=F32)
    z = jnp.exp(logits - jnp.max(logits, axis=-1, keepdims=True))
    aff_ref[...] = z / jnp.sum(z, axis=-1, keepdims=True)


def _outproj(o_na, o_df, x2, wo, g, wr, tm=512):
    n, d = x2.shape
    half = o_na.shape[-1]
    e = wr.shape[-1]
    return pl.pallas_call(
        _outproj_kernel,
        grid=(n // tm,),
        in_specs=[pl.BlockSpec((tm, half), lambda i: (i, 0)),
                  pl.BlockSpec((tm, half), lambda i: (i, 0)),
                  pl.BlockSpec((tm, d), lambda i: (i, 0)),
                  pl.BlockSpec((2 * half, d), lambda i: (0, 0)),
                  pl.BlockSpec((1, d), lambda i: (0, 0)),
                  pl.BlockSpec((d, e), lambda i: (0, 0))],
        out_specs=[pl.BlockSpec((tm, d), lambda i: (i, 0)),
                   pl.BlockSpec((tm, d), lambda i: (i, 0)),
                   pl.BlockSpec((tm, e), lambda i: (i, 0))],
        out_shape=[jax.ShapeDtypeStruct((n, d), F32),
                   jax.ShapeDtypeStruct((n, d), BF16),
                   jax.ShapeDtypeStruct((n, e), F32)],
        compiler_params=_cparams(("parallel",)),
        name="outproj",
    )(o_na, o_df, x2, wo, g, wr)


def _excl_prefix(mask, upper):
    e, s = mask.shape
    mb = jnp.where(mask, 1.0, 0.0).astype(BF16)
    carry = jnp.zeros((e, 1), F32)
    outs = []
    for c in range(s // LANES):
        xc = mb[:, c * LANES:(c + 1) * LANES]
        outs.append(jnp.dot(xc, upper, preferred_element_type=F32) + carry)
        carry = carry + jnp.sum(xc.astype(F32), axis=-1, keepdims=True)
    return jnp.concatenate(outs, axis=1)


def _route_kernel(aff_ref, slot_ref, gate_ref, pref_ref, *, cap):
    a = aff_ref[0]
    e = a.shape[0]

    def count_ge(t):
        return jnp.sum(jnp.where(a >= t, 1, 0), axis=-1, keepdims=True)

    def bit_body(i, bits):
        trial = bits | jnp.left_shift(jnp.int32(1), 30 - i)
        return jnp.where(count_ge(pltpu.bitcast(trial, F32)) >= cap, trial, bits)

    thr = pltpu.bitcast(lax.fori_loop(0, 31, bit_body, jnp.zeros((e, 1), jnp.int32)), F32)
    step0 = thr
    for j in range(1, 21):
        trial = thr + step0 * (2.0 ** -(24 + j))
        thr = jnp.where(count_ge(trial) >= cap, trial, thr)
    gt = a > thr
    eq = a == thr
    need = (cap - jnp.sum(jnp.where(gt, 1, 0), axis=-1, keepdims=True)).astype(F32)
    r = lax.broadcasted_iota(jnp.int32, (LANES, LANES), 0)
    c = lax.broadcasted_iota(jnp.int32, (LANES, LANES), 1)
    upper = jnp.where(r < c, 1.0, 0.0).astype(BF16)
    sel = gt | (eq & (_excl_prefix(eq, upper) < need))
    pref = _excl_prefix(sel, upper)
    pref_ref[0] = pref
    slot_ref[0] = jnp.where(sel, pref, -1.0)
    gate_ref[0] = jnp.where(sel, a, 0.0)


def _route(aff_t, cap):
    b, e, s = aff_t.shape
    spec = pl.BlockSpec((1, e, s), lambda i: (i, 0, 0))
    shp = jax.ShapeDtypeStruct((b, e, s), F32)
    return pl.pallas_call(
        functools.partial(_route_kernel, cap=cap),
        grid=(b,),
        in_specs=[spec],
        out_specs=[spec, spec, spec],
        out_shape=[shp, shp, shp],
        compiler_params=_cparams(("parallel",)),
        name="route",
    )(aff_t)


def _window_range(ts_ref, base, t):
    s0 = ts_ref[base + t]
    s1 = ts_ref[base + t + 1]
    w_lo = jnp.right_shift(s0, int(math.log2(MOE_WIN)))
    w_hi = jnp.where(s1 > s0, jnp.right_shift(s1 - 1, int(math.log2(MOE_WIN))) + 1, w_lo)
    return w_lo, w_hi


def _moe_ffn_kernel(ts_ref, xn_ref, slot_ref, gate_ref, wg_ref, wu_ref, wd_ref, y_ref, xs_acc, gs_acc,
                    *, n_tile, n_fchunk):
    b = pl.program_id(0)
    e = pl.program_id(1)
    base = (b * pl.num_programs(1) + e) * (n_tile + 1)
    xs_acc[...] = jnp.zeros_like(xs_acc)
    gs_acc[...] = jnp.zeros_like(gs_acc)
    sub = lax.broadcasted_iota(jnp.int32, (MOE_WIN, MOE_TILE), 0).astype(F32)
    last_win = xs_acc.shape[0] // MOE_WIN - 1

    def compact(w, srow, grow, xt):
        match = srow - (w * MOE_WIN).astype(F32) == sub
        onehot = jnp.where(match, 1.0, 0.0).astype(BF16)
        r0 = pl.multiple_of(jnp.minimum(w, last_win) * MOE_WIN, MOE_WIN)
        xs_acc[pl.ds(r0, MOE_WIN), :] += jnp.dot(onehot, xt, preferred_element_type=F32)
        gs_acc[pl.ds(r0, MOE_WIN), :] += jnp.sum(jnp.where(match, grow, 0.0), axis=-1, keepdims=True)

    def tile_body(t, carry):
        w_lo, w_hi = _window_range(ts_ref, base, t)
        srow = slot_ref[0, 0, t]
        grow = gate_ref[0, 0, t]
        xt = xn_ref[0, pl.ds(pl.multiple_of(t * MOE_TILE, MOE_TILE), MOE_TILE), :]

        def win_body(w, c2):
            compact(w, srow, grow, xt)
            return c2

        lax.fori_loop(w_lo + 2, w_hi, win_body, 0)
        return carry

    lax.fori_loop(0, n_tile, tile_body, 0)

    for t in range(n_tile):
        w_lo = jnp.right_shift(ts_ref[base + t], int(math.log2(MOE_WIN)))
        srow = slot_ref[0, 0, t]
        grow = gate_ref[0, 0, t]
        xt = xn_ref[0, t * MOE_TILE:(t + 1) * MOE_TILE, :]
        compact(w_lo, srow, grow, xt)
        compact(w_lo + 1, srow, grow, xt)

    xs = xs_acc[...].astype(BF16)
    f = wg_ref.shape[-1]
    fc = f // n_fchunk
    y = None
    for j in range(n_fchunk):
        fs = slice(j * fc, (j + 1) * fc)
        gt = jnp.dot(xs, wg_ref[0, :, fs], preferred_element_type=F32)
        up = jnp.dot(xs, wu_ref[0, :, fs], preferred_element_type=F32)
        hid = (gt * jax.nn.sigmoid(gt) * up).astype(BF16)
        part = jnp.dot(hid, wd_ref[0, fs, :], preferred_element_type=F32)
        y = part if y is None else y + part
    y_ref[0, 0] = (y * gs_acc[...]).astype(BF16)


def _moe_ffn(tstart, xn, slot_rows, gate_rows, wg, wu, wd, cap):
    b, s, d = xn.shape
    e, _, f = wg.shape
    n_tile = s // MOE_TILE
    row_spec = pl.BlockSpec((1, 1, n_tile, 1, MOE_TILE), lambda i, j, ts: (i, j, 0, 0, 0))
    grid_spec = pltpu.PrefetchScalarGridSpec(
        num_scalar_prefetch=1,
        grid=(b, e),
        in_specs=[pl.BlockSpec((1, s, d), lambda i, j, ts: (i, 0, 0)),
                  row_spec, row_spec,
                  pl.BlockSpec((1, d, f), lambda i, j, ts: (j, 0, 0)),
                  pl.BlockSpec((1, d, f), lambda i, j, ts: (j, 0, 0)),
                  pl.BlockSpec((1, f, d), lambda i, j, ts: (j, 0, 0))],
        out_specs=pl.BlockSpec((1, 1, cap, d), lambda i, j, ts: (i, j, 0, 0)),
        scratch_shapes=[pltpu.VMEM((cap, d), F32), pltpu.VMEM((cap, 1), F32)],
    )
    return pl.pallas_call(
        functools.partial(_moe_ffn_kernel, n_tile=n_tile, n_fchunk=max(1, f // 256)),
        grid_spec=grid_spec,
        out_shape=jax.ShapeDtypeStruct((b, e, cap, d), BF16),
        compiler_params=_cparams(("parallel", "arbitrary")),
        name="moe_ffn",
    )(tstart, xn, slot_rows, gate_rows, wg, wu, wd)


def _moe_comb_kernel(ts_ref, y_ref, slot_ref, o_ref, *, n_tile, win):
    b = pl.program_id(0)
    t = pl.program_id(1)
    n_exp, cap = y_ref.shape[1], y_ref.shape[2]
    lane = lax.broadcasted_iota(jnp.int32, (COMB_TILE, win), 1).astype(F32)
    acc = None
    for e in range(n_exp):
        s0 = ts_ref[(b * n_exp + e) * (n_tile + 1) + t]
        start = jnp.minimum(jnp.left_shift(jnp.right_shift(s0, 4), 4), cap - win)
        start = pl.multiple_of(start, 16)
        scol = jnp.broadcast_to(slot_ref[0, 0, e], (LANES, COMB_TILE)).T - start.astype(F32)
        scol = jnp.tile(scol, (1, win // LANES))
        onehot = jnp.where(scol == lane, 1.0, 0.0).astype(BF16)
        part = jnp.dot(onehot, y_ref[0, e, pl.ds(start, win), :], preferred_element_type=F32)
        acc = part if acc is None else acc + part
    o_ref[0] = acc


def _moe_combine(tstart, y, slot_rows, s):
    b, e, cap, d = y.shape
    n_tile = s // COMB_TILE
    win = min(cap, 2 * COMB_TILE)
    assert win == cap or win >= COMB_TILE + 15
    grid_spec = pltpu.PrefetchScalarGridSpec(
        num_scalar_prefetch=1,
        grid=(b, n_tile),
        in_specs=[pl.BlockSpec((1, e, cap, d), lambda i, j, ts: (i, 0, 0, 0)),
                  pl.BlockSpec((1, 1, e, 1, COMB_TILE), lambda i, j, ts: (i, j, 0, 0, 0))],
        out_specs=pl.BlockSpec((1, COMB_TILE, d), lambda i, j, ts: (i, j, 0)),
    )
    return pl.pallas_call(
        functools.partial(_moe_comb_kernel, n_tile=n_tile, win=win),
        grid_spec=grid_spec,
        out_shape=jax.ShapeDtypeStruct((b, s, d), F32),
        compiler_params=_cparams(("parallel", "arbitrary")),
        name="moe_comb",
    )(tstart, y, slot_rows)


def _ple_kernel(h1_ref, moe_ref, p_ref, gp_ref, wg_ref, wp_ref, gf_ref, o_ref, *, last_layer):
    h2 = h1_ref[...] + moe_ref[...]
    hn = _rms(h2, gp_ref[...]).astype(BF16)
    gate = jax.nn.sigmoid(jnp.dot(hn, wg_ref[...], preferred_element_type=F32))
    proj = jnp.dot(p_ref[...].astype(BF16), wp_ref[...], preferred_element_type=F32)
    h3 = h2 + gate * proj
    o_ref[...] = _rms(h3, gf_ref[...]) if last_layer else h3


def _ple(h1, moe, p2, gp, wg, wp, gf, last_layer, tm=512):
    n, d = h1.shape
    dp = p2.shape[-1]
    row = pl.BlockSpec((tm, d), lambda i: (i, 0))
    vec = pl.BlockSpec((1, d), lambda i: (0, 0))
    return pl.pallas_call(
        functools.partial(_ple_kernel, last_layer=last_layer),
        grid=(n // tm,),
        in_specs=[row, row, pl.BlockSpec((tm, dp), lambda i: (i, 0)), vec,
                  pl.BlockSpec((d, d), lambda i: (0, 0)), pl.BlockSpec((dp, d), lambda i: (0, 0)), vec],
        out_specs=row,
        out_shape=jax.ShapeDtypeStruct((n, d), F32),
        compiler_params=_cparams(("parallel",)),
        name="ple",
    )(h1, moe, p2, gp, wg, wp, gf)


def kernel(x, p, g_mix, w_in, na_rpb, g_na_out, lam_q1, lam_k1, lam_q2, lam_k2, g_diff_out, w_out,
           g_moe, w_router, w_gate, w_up, w_down, g_ple, w_ple_gate, w_ple_proj, g_final):
    b, s, d = x.shape
    depth = w_in.shape[0]
    n = b * s
    na_w = NA_HEADS * HEAD_DIM
    cap = EC_CAPACITY_FACTOR * s // N_EXPERTS
    n_tile = s // MOE_TILE
    col_scale = np.ones((w_in.shape[-1],), np.float32)
    col_scale[:na_w] = HEAD_DIM ** -0.5 * LOG2E
    col_scale[3 * na_w:3 * na_w + DIFF_HEADS * 2 * HEAD_DIM] = HEAD_DIM ** -0.5 * LOG2E

    h = x.reshape(n, d)
    for i in range(depth):
        lam_init = 0.8 - 0.6 * math.exp(-0.3 * i)
        proj = _inproj(h, g_mix[i][None], (w_in[i] * col_scale).astype(BF16)).reshape(b, s, -1)
        o_na = _na_attention(proj, _na_bias_table(na_rpb[i]), g_na_out[i].reshape(1, na_w))
        lam_vecs = jnp.stack([lam_q1[i], lam_k1[i], lam_q2[i], lam_k2[i]]).astype(F32)
        o_df = _df_attention(proj, lam_vecs, g_diff_out[i][None], lam_init)
        h1, xn, aff = _outproj(o_na.reshape(n, -1), o_df.reshape(n, -1), h, w_out[i].astype(BF16),
                               g_moe[i][None], w_router[i].astype(BF16))
        aff_t = aff.reshape(b, s, N_EXPERTS).transpose(0, 2, 1)
        slot, gate, pref = _route(aff_t, cap)
        last = jnp.full((b, N_EXPERTS, 1), cap, F32)
        tstart = jnp.concatenate([pref[:, :, ::MOE_TILE], last], axis=-1).astype(jnp.int32).reshape(-1)
        cstart = jnp.concatenate([pref[:, :, ::COMB_TILE], last], axis=-1).astype(jnp.int32).reshape(-1)
        slot_rows = slot.reshape(b, N_EXPERTS, n_tile, 1, MOE_TILE)
        gate_rows = gate.reshape(b, N_EXPERTS, n_tile, 1, MOE_TILE)
        y = _moe_ffn(tstart, xn.reshape(b, s, d), slot_rows, gate_rows, w_gate[i].astype(BF16),
                     w_up[i].astype(BF16), w_down[i].astype(BF16), cap)
        comb_rows = slot.reshape(b, N_EXPERTS, s // COMB_TILE, 1, COMB_TILE).transpose(0, 2, 1, 3, 4)
        moe = _moe_combine(cstart, y, comb_rows, s)
        h = _ple(h1, moe.reshape(n, d), p[i].reshape(n, -1), g_ple[i][None],
                 w_ple_gate[i].astype(BF16), w_ple_proj[i].astype(BF16), g_final[None],
                 last_layer=(i == depth - 1))
    return h.reshape(b, s, d)
```

```python
import functools
import math

import jax
import jax.numpy as jnp
import numpy as np
from jax import lax
from jax.experimental import pallas as pl
from jax.experimental.pallas import tpu as pltpu

F32 = jnp.float32
BF16 = jnp.bfloat16

EPS = 1e-6
GRID_W = 64
HEAD_DIM = 64
NA_HEADS = 8
NA_KH_MAX = 8
NA_KW = 16
DIFF_HEADS = 4
N_EXPERTS = 16
EC_CAPACITY_FACTOR = 2
LANES = 128
MOE_TILE = 256
MOE_WIN = 128
COMB_TILE = 128
COMB_PACK = 64
NEG_BIG = -1e30
LOG2E = math.log2(math.e)
VMEM_LIMIT = 56 * 1024 * 1024


def _cparams(sem):
    return pltpu.CompilerParams(dimension_semantics=sem, vmem_limit_bytes=VMEM_LIMIT)


def _rms(x, g):
    return x * lax.rsqrt(jnp.mean(x * x, axis=-1, keepdims=True) + EPS) * g


def _inproj_kernel(x_ref, g_ref, w_ref, o_ref, *, n_chunk):
    xn = _rms(x_ref[...], g_ref[...]).astype(BF16)
    cw = o_ref.shape[-1] // n_chunk
    for j in range(n_chunk):
        o_ref[:, j * cw:(j + 1) * cw] = jnp.dot(
            xn, w_ref[:, j * cw:(j + 1) * cw], preferred_element_type=F32).astype(BF16)


def _inproj(x2, g, w, tm=512):
    n, d = x2.shape
    nout = w.shape[1]
    return pl.pallas_call(
        functools.partial(_inproj_kernel, n_chunk=nout // 512),
        grid=(n // tm,),
        in_specs=[pl.BlockSpec((tm, d), lambda i: (i, 0)),
                  pl.BlockSpec((1, d), lambda i: (0, 0)),
                  pl.BlockSpec((d, nout), lambda i: (0, 0))],
        out_specs=pl.BlockSpec((tm, nout), lambda i: (i, 0)),
        out_shape=jax.ShapeDtypeStruct((n, nout), BF16),
        compiler_params=_cparams(("parallel",)),
        name="inproj",
    )(x2, g, w)


def _na_kernel(q_ref, k_ref, v_ref, bias_ref, g_ref, o_ref,
               s0_scr, s1_scr, p0_scr, p1_scr, l0_scr, l1_scr, *, rows, kh, rb):
    lane = lax.broadcasted_iota(jnp.int32, (GRID_W, LANES), 1)
    lo = lane < HEAD_DIM
    nkeys = kh * GRID_W
    n_pair = NA_HEADS // 2
    zero = jnp.zeros((GRID_W, LANES), BF16)

    def geometry(r):
        rs = jnp.clip(r - kh // 2, 0, rows - kh)
        return (pl.multiple_of(r * GRID_W, GRID_W), pl.multiple_of(rs * GRID_W, GRID_W),
                rs - r + NA_KH_MAX - 1)

    def scores(r, s_scr):
        q0, k0, _ = geometry(r)
        for j in range(n_pair):
            cs = slice(j * LANES, (j + 1) * LANES)
            qj = q_ref[0, pl.ds(q0, GRID_W), cs]
            q2 = jnp.concatenate([jnp.where(lo, qj, zero), jnp.where(lo, zero, qj)], axis=0)
            s_scr[j] = lax.dot_general(q2, k_ref[0, pl.ds(k0, nkeys), cs], (((1,), (1,)), ((), ())),
                                       preferred_element_type=F32)

    def softmax(r, s_scr, p_scr, l_scr):
        _, _, off = geometry(r)
        for j in range(n_pair):
            for sb in range(2 * GRID_W // rb):
                rws = slice(sb * rb, (sb + 1) * rb)
                head = 2 * j + (sb * rb) // GRID_W
                brow = slice((sb * rb) % GRID_W, (sb * rb) % GRID_W + rb)
                chunks = [s_scr[j, rws, i * LANES:(i + 1) * LANES] + bias_ref[head, off + 2 * i, brow, :]
                          for i in range(kh // 2)]
                m = chunks[0]
                for c in chunks[1:]:
                    m = jnp.maximum(m, c)
                m = jnp.max(m, axis=-1, keepdims=True)
                lsum = None
                for i, c in enumerate(chunks):
                    pi = jnp.exp2(c - m)
                    p_scr[j, rws, i * LANES:(i + 1) * LANES] = pi.astype(BF16)
                    lsum = pi if lsum is None else lsum + pi
                l_scr[j, rws, :] = lsum

    def values(r, p_scr, l_scr):
        q0, k0, _ = geometry(r)
        for j in range(n_pair):
            cs = slice(j * LANES, (j + 1) * LANES)
            of = jnp.dot(p_scr[j], v_ref[0, pl.ds(k0, nkeys), cs], preferred_element_type=F32)
            of = of / jnp.sum(l_scr[j], axis=-1, keepdims=True)
            o = jnp.where(lo, of[:GRID_W], of[GRID_W:])
            sq = o * o
            ms_a = jnp.sum(jnp.where(lo, sq, 0.0), axis=-1, keepdims=True) * (1.0 / HEAD_DIM)
            ms_b = jnp.sum(jnp.where(lo, 0.0, sq), axis=-1, keepdims=True) * (1.0 / HEAD_DIM)
            inv = jnp.where(lo, lax.rsqrt(ms_a + EPS), lax.rsqrt(ms_b + EPS))
            o_ref[0, pl.ds(q0, GRID_W), cs] = (o * inv * g_ref[:, cs]).astype(BF16)

    p1_scr[...] = jnp.zeros_like(p1_scr)
    l1_scr[...] = jnp.ones_like(l1_scr)
    scores(0, s0_scr)

    def pair_body(i, carry):
        r = 2 * i
        scores(r + 1, s1_scr)
        softmax(r, s0_scr, p0_scr, l0_scr)
        values(jnp.maximum(r - 1, 0), p1_scr, l1_scr)
        scores(jnp.minimum(r + 2, rows - 1), s0_scr)
        softmax(r + 1, s1_scr, p1_scr, l1_scr)
        values(r, p0_scr, l0_scr)
        return carry

    lax.fori_loop(0, rows // 2, pair_body, 0)
    values(rows - 1, p1_scr, l1_scr)


def _na_attention(proj, bias, g, rb=32):
    b, s, _ = proj.shape
    rows = s // GRID_W
    kh = min(NA_KH_MAX, rows)
    assert rows % 2 == 0 and kh % 2 == 0 and GRID_W % rb == 0
    w = NA_HEADS * HEAD_DIM
    n_pair = NA_HEADS // 2
    stacked = 2 * GRID_W
    return pl.pallas_call(
        functools.partial(_na_kernel, rows=rows, kh=kh, rb=rb),
        grid=(b,),
        in_specs=[pl.BlockSpec((1, s, w), lambda i: (i, 0, 0)),
                  pl.BlockSpec((1, s, w), lambda i: (i, 0, 1)),
                  pl.BlockSpec((1, s, w), lambda i: (i, 0, 2)),
                  pl.BlockSpec(bias.shape, lambda i: (0, 0, 0, 0)),
                  pl.BlockSpec((1, w), lambda i: (0, 0))],
        out_specs=pl.BlockSpec((1, s, w), lambda i: (i, 0, 0)),
        out_shape=jax.ShapeDtypeStruct((b, s, w), BF16),
        scratch_shapes=[pltpu.VMEM((n_pair, stacked, kh * GRID_W), F32),
                        pltpu.VMEM((n_pair, stacked, kh * GRID_W), F32),
                        pltpu.VMEM((n_pair, stacked, kh * GRID_W), BF16),
                        pltpu.VMEM((n_pair, stacked, kh * GRID_W), BF16),
                        pltpu.VMEM((n_pair, stacked, LANES), F32),
                        pltpu.VMEM((n_pair, stacked, LANES), F32)],
        compiler_params=_cparams(("parallel",)),
        name="na_attn",
    )(proj, proj, proj, bias, g)


def _na_bias_table(rpb):
    qcol = np.arange(GRID_W)[:, None]
    kcol = np.arange(GRID_W)[None, :]
    cstart = np.clip(qcol - NA_KW // 2, 0, GRID_W - NA_KW)
    ok = (kcol >= cstart) & (kcol < cstart + NA_KW)
    dc = np.clip(kcol - qcol + NA_KW - 1, 0, 2 * NA_KW - 2)
    t = jnp.where(jnp.asarray(ok)[None, None], rpb.astype(F32)[:, :, dc] * LOG2E, NEG_BIG)
    return jnp.concatenate([t[:, :-1], t[:, 1:]], axis=-1)


def _split3(x):
    hi = x.astype(BF16).astype(F32)
    rest = x - hi
    mid = rest.astype(BF16).astype(F32)
    return hi, mid, rest - mid


def _df_kernel(q_ref, k_ref, v_ref, lam_ref, g_ref, o_ref,
               bias_scr, kaug_scr, q2_scr, s0_scr, s1_scr, p0_scr, p1_scr, a0_scr, a1_scr,
               m_scr, l_scr, acc_scr, *, tq, tk, rb, s_len, lam_init):
    h = pl.program_id(1)
    qi = pl.program_id(2)
    slope = jnp.exp2(-8.0 * (h + 1).astype(F32) / DIFF_HEADS) * LOG2E
    n_kb = s_len // tk
    n_par = tk // tq
    nt = (((1,), (1,)), ((), ()))

    @pl.when(qi == 0)
    def _():
        a = lax.broadcasted_iota(jnp.int32, (tq, tk), 0)
        b = lax.broadcasted_iota(jnp.int32, (tq, tk), 1)
        dist = (a - b).astype(F32)
        for t in range(n_par):
            bias_scr[t] = slope * jnp.abs(dist + float(t * tq))
        klane = lax.broadcasted_iota(jnp.int32, (tk, LANES), 1)
        hi, mid, low = _split3(slope * lax.broadcasted_iota(jnp.int32, (tk, LANES), 0).astype(F32))
        kaug = jnp.where(klane < 3, 1.0, jnp.where(klane == 3, hi, jnp.where(
            klane == 4, mid, jnp.where(klane == 5, low, 0.0))))
        kaug_scr[...] = kaug.astype(BF16)

    lane = lax.broadcasted_iota(jnp.int32, (tq, LANES), 1)
    lo = lane < HEAD_DIM
    q = q_ref[0]
    zero = jnp.zeros_like(q)
    qm = (jnp.where(lo, q, zero), jnp.where(lo, zero, q))
    q2d = jnp.concatenate(qm, axis=0)
    hi, mid, low = _split3(-slope * lax.broadcasted_iota(jnp.int32, (tq, LANES), 0).astype(F32))
    qaug = jnp.where(lane == 0, hi, jnp.where(lane == 1, mid, jnp.where(
        lane == 2, low, jnp.where(lane < 6, 1.0, 0.0))))
    for v, sign in enumerate((1.0, -1.0)):
        aug = (sign * qaug).astype(BF16)
        q2_scr[v] = jnp.concatenate([jnp.concatenate([qm[0], aug], axis=1),
                                     jnp.concatenate([qm[1], aug], axis=1)], axis=0)

    k_diag = jnp.right_shift(qi, int(math.log2(n_par)))
    par = jnp.bitwise_and(qi, n_par - 1)

    def block(r):
        kb = k_diag + r
        wrapped = kb >= n_kb
        kb = jnp.where(wrapped, kb - n_kb, kb)
        return pl.multiple_of(kb * tk, tk), wrapped

    def scores(r, s_scr):
        k0, wrapped = block(r)
        k_blk = k_ref[0, pl.ds(k0, tk), :]
        if r == 0:
            s_scr[...] = lax.dot_general(q2d, k_blk, nt, preferred_element_type=F32)
        else:
            s_scr[...] = lax.dot_general(q2_scr[jnp.where(wrapped, 0, 1)],
                                         jnp.concatenate([k_blk, kaug_scr[...]], axis=1), nt,
                                         preferred_element_type=F32)

    def softmax(r, s_scr, p_scr, a_scr):
        k0, wrapped = block(r)
        if r == 0:
            beta = 0.0
        else:
            cf = slope * (qi * tq - k0).astype(F32)
            beta = jnp.where(wrapped, -cf, cf)
        for sb in range(2 * tq // rb):
            rows = slice(sb * rb, (sb + 1) * rb)
            brow = slice((sb * rb) % tq, (sb * rb) % tq + rb)
            s = s_scr[rows, :]
            if r == 0:
                s = s - bias_scr[par, brow, :]
            m_old = m_scr[rows, :]
            m_new = jnp.maximum(m_old, jnp.max(s, axis=-1, keepdims=True) + beta)
            alpha = jnp.exp2(m_old - m_new)
            shift = m_new - beta
            psum = None
            for j in range(tk // LANES):
                cols = slice(j * LANES, (j + 1) * LANES)
                pj = jnp.exp2(s[:, cols] - shift)
                p_scr[rows, cols] = pj.astype(BF16)
                psum = pj if psum is None else psum + pj
            l_scr[rows, :] = alpha * l_scr[rows, :] + psum
            m_scr[rows, :] = m_new
            a_scr[rows, :] = alpha

    def values(r, p_scr, a_scr):
        k0, _ = block(r)
        acc_scr[...] = a_scr[...] * acc_scr[...] + jnp.dot(
            p_scr[...], v_ref[0, pl.ds(k0, tk), :], preferred_element_type=F32)

    m_scr[...] = jnp.full_like(m_scr, NEG_BIG)
    l_scr[...] = jnp.zeros_like(l_scr)
    acc_scr[...] = jnp.zeros_like(acc_scr)
    s_bufs, p_bufs, a_bufs = (s0_scr, s1_scr), (p0_scr, p1_scr), (a0_scr, a1_scr)
    scores(0, s0_scr)
    for r in range(n_kb):
        cur, oth = r % 2, 1 - r % 2
        if r + 1 < n_kb:
            scores(r + 1, s_bufs[oth])
        softmax(r, s_bufs[cur], p_bufs[cur], a_bufs[cur])
        if r >= 1:
            values(r - 1, p_bufs[oth], a_bufs[oth])
    values(n_kb - 1, p_bufs[(n_kb - 1) % 2], a_bufs[(n_kb - 1) % 2])
    o2 = acc_scr[...] / jnp.sum(l_scr[...], axis=-1, keepdims=True)
    lv = lam_ref[...]
    lam = (jnp.exp(jnp.sum(lv[0:1] * lv[1:2], axis=-1, keepdims=True))
           - jnp.exp(jnp.sum(lv[2:3] * lv[3:4], axis=-1, keepdims=True)) + lam_init)
    o = o2[:tq] - lam * o2[tq:]
    o_ref[0] = (_rms(o, g_ref[...]) * (1.0 - lam_init)).astype(BF16)


def _df_attention(proj, lam_vecs, g, lam_init, tq=256, tk=512, rb=32):
    b, s, _ = proj.shape
    tq = min(tq, s)
    tk = min(tk, s)
    assert tk % tq == 0
    qb, kb, vb = 12, 16, 20
    return pl.pallas_call(
        functools.partial(_df_kernel, tq=tq, tk=tk, rb=rb, s_len=s, lam_init=lam_init),
        grid=(b, DIFF_HEADS, s // tq),
        in_specs=[pl.BlockSpec((1, tq, LANES), lambda i, h, j: (i, j, qb + h)),
                  pl.BlockSpec((1, s, LANES), lambda i, h, j: (i, 0, kb + h)),
                  pl.BlockSpec((1, s, LANES), lambda i, h, j: (i, 0, vb + h)),
                  pl.BlockSpec((4, HEAD_DIM), lambda i, h, j: (0, 0)),
                  pl.BlockSpec((1, LANES), lambda i, h, j: (0, 0))],
        out_specs=pl.BlockSpec((1, tq, LANES), lambda i, h, j: (i, j, h)),
        out_shape=jax.ShapeDtypeStruct((b, s, DIFF_HEADS * LANES), BF16),
        scratch_shapes=[pltpu.VMEM((tk // tq, tq, tk), F32),
                        pltpu.VMEM((tk, LANES), BF16),
                        pltpu.VMEM((2, 2 * tq, 2 * LANES), BF16),
                        pltpu.VMEM((2 * tq, tk), F32),
                        pltpu.VMEM((2 * tq, tk), F32),
                        pltpu.VMEM((2 * tq, tk), BF16),
                        pltpu.VMEM((2 * tq, tk), BF16),
                        pltpu.VMEM((2 * tq, LANES), F32),
                        pltpu.VMEM((2 * tq, LANES), F32),
                        pltpu.VMEM((2 * tq, LANES), F32),
                        pltpu.VMEM((2 * tq, LANES), F32),
                        pltpu.VMEM((2 * tq, LANES), F32)],
        compiler_params=_cparams(("parallel", "parallel", "arbitrary")),
        name="df_attn",
    )(proj, proj, proj, lam_vecs, g)


def _outproj_kernel(ona_ref, odf_ref, x_ref, wo_ref, g_ref, wr_ref, h1_ref, xn_ref, aff_ref):
    half = ona_ref.shape[-1]
    acc = jnp.dot(ona_ref[...], wo_ref[:half, :], preferred_element_type=F32)
    acc = acc + jnp.dot(odf_ref[...], wo_ref[half:, :], preferred_element_type=F32)
    h1 = x_ref[...] + acc
    h1_ref[...] = h1
    xn = _rms(h1, g_ref[...]).astype(BF16)
    xn_ref[...] = xn
    logits = jnp.dot(xn, wr_ref[...], preferred_element_type=F32)
    z = jnp.exp(logits - jnp.max(logits, axis=-1, keepdims=True))
    aff_ref[...] = z / jnp.sum(z, axis=-1, keepdims=True)


def _outproj(o_na, o_df, x2, wo, g, wr, tm=1024):
    n, d = x2.shape
    half = o_na.shape[-1]
    e = wr.shape[-1]
    return pl.pallas_call(
        _outproj_kernel,
        grid=(n // tm,),
        in_specs=[pl.BlockSpec((tm, half), lambda i: (i, 0)),
                  pl.BlockSpec((tm, half), lambda i: (i, 0)),
                  pl.BlockSpec((tm, d), lambda i: (i, 0)),
                  pl.BlockSpec((2 * half, d), lambda i: (0, 0)),
                  pl.BlockSpec((1, d), lambda i: (0, 0)),
                  pl.BlockSpec((d, e), lambda i: (0, 0))],
        out_specs=[pl.BlockSpec((tm, d), lambda i: (i, 0)),
                   pl.BlockSpec((tm, d), lambda i: (i, 0)),
                   pl.BlockSpec((tm, e), lambda i: (i, 0))],
        out_shape=[jax.ShapeDtypeStruct((n, d), F32),
                   jax.ShapeDtypeStruct((n, d), BF16),
                   jax.ShapeDtypeStruct((n, e), F32)],
        compiler_params=_cparams(("parallel",)),
        name="outproj",
    )(o_na, o_df, x2, wo, g, wr)


def _excl_prefix(mask, upper):
    e, s = mask.shape
    mb = jnp.where(mask, 1.0, 0.0).astype(BF16)
    carry = jnp.zeros((e, 1), F32)
    outs = []
    for c in range(s // LANES):
        xc = mb[:, c * LANES:(c + 1) * LANES]
        outs.append(jnp.dot(xc, upper, preferred_element_type=F32) + carry)
        carry = carry + jnp.sum(xc.astype(F32), axis=-1, keepdims=True)
    return jnp.concatenate(outs, axis=1)


def _route_kernel(aff_ref, slot_ref, gate_ref, pref_ref, *, cap):
    a = aff_ref[0]
    e = a.shape[0]

    def count_ge(t):
        return jnp.sum(jnp.where(a >= t, 1, 0), axis=-1, keepdims=True)

    def bit_body(i, bits):
        trial = bits | jnp.left_shift(jnp.int32(1), 30 - i)
        return jnp.where(count_ge(pltpu.bitcast(trial, F32)) >= cap, trial, bits)

    thr = pltpu.bitcast(lax.fori_loop(0, 31, bit_body, jnp.zeros((e, 1), jnp.int32)), F32)
    step0 = thr
    for j in range(1, 21):
        trial = thr + step0 * (2.0 ** -(24 + j))
        thr = jnp.where(count_ge(trial) >= cap, trial, thr)
    gt = a > thr
    eq = a == thr
    need = (cap - jnp.sum(jnp.where(gt, 1, 0), axis=-1, keepdims=True)).astype(F32)
    r = lax.broadcasted_iota(jnp.int32, (LANES, LANES), 0)
    c = lax.broadcasted_iota(jnp.int32, (LANES, LANES), 1)
    upper = jnp.where(r < c, 1.0, 0.0).astype(BF16)
    sel = gt | (eq & (_excl_prefix(eq, upper) < need))
    pref = _excl_prefix(sel, upper)
    pref_ref[0] = pref
    slot_ref[0] = jnp.where(sel, pref, -1.0)
    gate_ref[0] = jnp.where(sel, a, 0.0)


def _route(aff_t, cap):
    b, e, s = aff_t.shape
    spec = pl.BlockSpec((1, e, s), lambda i: (i, 0, 0))
    shp = jax.ShapeDtypeStruct((b, e, s), F32)
    return pl.pallas_call(
        functools.partial(_route_kernel, cap=cap),
        grid=(b,),
        in_specs=[spec],
        out_specs=[spec, spec, spec],
        out_shape=[shp, shp, shp],
        compiler_params=_cparams(("parallel",)),
        name="route",
    )(aff_t)


def _window_range(ts_ref, base, t):
    s0 = ts_ref[base + t]
    s1 = ts_ref[base + t + 1]
    w_lo = jnp.right_shift(s0, int(math.log2(MOE_WIN)))
    w_hi = jnp.where(s1 > s0, jnp.right_shift(s1 - 1, int(math.log2(MOE_WIN))) + 1, w_lo)
    return w_lo, w_hi


def _moe_ffn_kernel(ts_ref, xn_ref, slot_ref, gate_ref, wg_ref, wu_ref, wd_ref, y_ref, xs_acc, gs_acc,
                    *, n_tile, n_fchunk):
    b = pl.program_id(0)
    e = pl.program_id(1)
    base = (b * pl.num_programs(1) + e) * (n_tile + 1)
    xs_acc[...] = jnp.zeros_like(xs_acc)
    gs_acc[...] = jnp.zeros_like(gs_acc)
    sub = lax.broadcasted_iota(jnp.int32, (MOE_WIN, MOE_TILE), 0).astype(F32)
    last_win = xs_acc.shape[0] // MOE_WIN - 1

    def compact(w, srow, grow, xt):
        match = srow - (w * MOE_WIN).astype(F32) == sub
        onehot = jnp.where(match, 1.0, 0.0).astype(BF16)
        r0 = pl.multiple_of(jnp.minimum(w, last_win) * MOE_WIN, MOE_WIN)
        xs_acc[pl.ds(r0, MOE_WIN), :] += jnp.dot(onehot, xt, preferred_element_type=F32)
        gs_acc[pl.ds(r0, MOE_WIN), :] += jnp.sum(jnp.where(match, grow, 0.0), axis=-1, keepdims=True)

    def tile_body(t, carry):
        w_lo, w_hi = _window_range(ts_ref, base, t)
        srow = slot_ref[0, 0, t]
        grow = gate_ref[0, 0, t]
        xt = xn_ref[0, pl.ds(pl.multiple_of(t * MOE_TILE, MOE_TILE), MOE_TILE), :]

        def win_body(w, c2):
            compact(w, srow, grow, xt)
            return c2

        lax.fori_loop(w_lo + 2, w_hi, win_body, 0)
        return carry

    lax.fori_loop(0, n_tile, tile_body, 0)

    for t in range(n_tile):
        w_lo = jnp.right_shift(ts_ref[base + t], int(math.log2(MOE_WIN)))
        srow = slot_ref[0, 0, t]
        grow = gate_ref[0, 0, t]
        xt = xn_ref[0, t * MOE_TILE:(t + 1) * MOE_TILE, :]
        compact(w_lo, srow, grow, xt)
        compact(w_lo + 1, srow, grow, xt)

    xs = xs_acc[...].astype(BF16)
    f = wg_ref.shape[-1]
    fc = f // n_fchunk
    y = None
    for j in range(n_fchunk):
        fs = slice(j * fc, (j + 1) * fc)
        gt = jnp.dot(xs, wg_ref[0, :, fs], preferred_element_type=F32)
        up = jnp.dot(xs, wu_ref[0, :, fs], preferred_element_type=F32)
        hid = (gt * jax.nn.sigmoid(gt) * up).astype(BF16)
        part = jnp.dot(hid, wd_ref[0, fs, :], preferred_element_type=F32)
        y = part if y is None else y + part
    y_ref[0, 0] = (y * gs_acc[...]).astype(BF16)


def _moe_ffn(tstart, xn, slot_rows, gate_rows, wg, wu, wd, cap):
    b, s, d = xn.shape
    e, _, f = wg.shape
    n_tile = s // MOE_TILE
    row_spec = pl.BlockSpec((1, 1, n_tile, 1, MOE_TILE), lambda i, j, ts: (i, j, 0, 0, 0))
    grid_spec = pltpu.PrefetchScalarGridSpec(
        num_scalar_prefetch=1,
        grid=(b, e),
        in_specs=[pl.BlockSpec((1, s, d), lambda i, j, ts: (i, 0, 0)),
                  row_spec, row_spec,
                  pl.BlockSpec((1, d, f), lambda i, j, ts: (j, 0, 0)),
                  pl.BlockSpec((1, d, f), lambda i, j, ts: (j, 0, 0)),
                  pl.BlockSpec((1, f, d), lambda i, j, ts: (j, 0, 0))],
        out_specs=pl.BlockSpec((1, 1, cap, d), lambda i, j, ts: (i, j, 0, 0)),
        scratch_shapes=[pltpu.VMEM((cap, d), F32), pltpu.VMEM((cap, 1), F32)],
    )
    return pl.pallas_call(
        functools.partial(_moe_ffn_kernel, n_tile=n_tile, n_fchunk=max(1, f // 256)),
        grid_spec=grid_spec,
        out_shape=jax.ShapeDtypeStruct((b, e, cap, d), BF16),
        compiler_params=_cparams(("parallel", "arbitrary")),
        name="moe_ffn",
    )(tstart, xn, slot_rows, gate_rows, wg, wu, wd)


def _window_start(s0, cap, win):
    return jnp.minimum(jnp.left_shift(jnp.right_shift(s0, 4), 4), cap - win)


def _moe_comb_kernel(ts_ref, y_ref, slot_ref, rel_ref, o_ref, ycat_scr, *, n_tile, win):
    b = pl.program_id(0)
    t = pl.program_id(1)
    n_exp, cap = y_ref.shape[1], y_ref.shape[2]

    def first_slot(e):
        return ts_ref[(b * n_exp + e) * (n_tile + 1) + t]

    n_max = ts_ref[b * n_exp * (n_tile + 1) + t + 1] - first_slot(0)
    for e in range(1, n_exp):
        n_max = jnp.maximum(n_max, ts_ref[(b * n_exp + e) * (n_tile + 1) + t + 1] - first_slot(e))
    packed = n_max <= COMB_PACK - 15

    @pl.when(packed)
    def _():
        for e in range(n_exp):
            start = pl.multiple_of(_window_start(first_slot(e), cap, COMB_PACK), 16)
            ycat_scr[e * COMB_PACK:(e + 1) * COMB_PACK, :] = y_ref[0, e, pl.ds(start, COMB_PACK), :]
        width = n_exp * COMB_PACK
        shift = int(math.log2(COMB_PACK))
        lane_e = jnp.right_shift(lax.broadcasted_iota(jnp.int32, (n_exp, width), 1), shift)
        spread = jnp.where(lane_e == lax.broadcasted_iota(jnp.int32, (n_exp, width), 0), 1.0, 0.0)
        rel = jnp.dot(rel_ref[0], spread.astype(BF16), preferred_element_type=F32)
        lane_j = jnp.bitwise_and(lax.broadcasted_iota(jnp.int32, (COMB_TILE, width), 1), COMB_PACK - 1)
        onehot = jnp.where(rel == lane_j.astype(F32), 1.0, 0.0).astype(BF16)
        o_ref[0] = jnp.dot(onehot, ycat_scr[...], preferred_element_type=F32)

    @pl.when(jnp.logical_not(packed))
    def _():
        lane = lax.broadcasted_iota(jnp.int32, (COMB_TILE, win), 1).astype(F32)
        acc = None
        for e in range(n_exp):
            start = pl.multiple_of(_window_start(first_slot(e), cap, win), 16)
            scol = jnp.broadcast_to(slot_ref[0, 0, e], (LANES, COMB_TILE)).T - start.astype(F32)
            scol = jnp.tile(scol, (1, win // LANES))
            onehot = jnp.where(scol == lane, 1.0, 0.0).astype(BF16)
            part = jnp.dot(onehot, y_ref[0, e, pl.ds(start, win), :], preferred_element_type=F32)
            acc = part if acc is None else acc + part
        o_ref[0] = acc


def _moe_combine(tstart, y, slot_rows, rel, s):
    b, e, cap, d = y.shape
    n_tile = s // COMB_TILE
    win = min(cap, 2 * COMB_TILE)
    assert win == cap or win >= COMB_TILE + 15
    grid_spec = pltpu.PrefetchScalarGridSpec(
        num_scalar_prefetch=1,
        grid=(b, n_tile),
        in_specs=[pl.BlockSpec((1, e, cap, d), lambda i, j, ts: (i, 0, 0, 0)),
                  pl.BlockSpec((1, 1, e, 1, COMB_TILE), lambda i, j, ts: (i, j, 0, 0, 0)),
                  pl.BlockSpec((1, COMB_TILE, e), lambda i, j, ts: (i, j, 0))],
        out_specs=pl.BlockSpec((1, COMB_TILE, d), lambda i, j, ts: (i, j, 0)),
        scratch_shapes=[pltpu.VMEM((e * COMB_PACK, d), BF16)],
    )
    return pl.pallas_call(
        functools.partial(_moe_comb_kernel, n_tile=n_tile, win=win),
        grid_spec=grid_spec,
        out_shape=jax.ShapeDtypeStruct((b, s, d), F32),
        compiler_params=_cparams(("parallel", "arbitrary")),
        name="moe_comb",
    )(tstart, y, slot_rows, rel)


def _ple_kernel(h1_ref, moe_ref, p_ref, gp_ref, wg_ref, wp_ref, gf_ref, o_ref, *, last_layer):
    h2 = h1_ref[...] + moe_ref[...]
    hn = _rms(h2, gp_ref[...]).astype(BF16)
    gate = jax.nn.sigmoid(jnp.dot(hn, wg_ref[...], preferred_element_type=F32))
    proj = jnp.dot(p_ref[...].astype(BF16), wp_ref[...], preferred_element_type=F32)
    h3 = h2 + gate * proj
    o_ref[...] = _rms(h3, gf_ref[...]) if last_layer else h3


def _ple(h1, moe, p2, gp, wg, wp, gf, last_layer, tm=1024):
    n, d = h1.shape
    dp = p2.shape[-1]
    row = pl.BlockSpec((tm, d), lambda i: (i, 0))
    vec = pl.BlockSpec((1, d), lambda i: (0, 0))
    return pl.pallas_call(
        functools.partial(_ple_kernel, last_layer=last_layer),
        grid=(n // tm,),
        in_specs=[row, row, pl.BlockSpec((tm, dp), lambda i: (i, 0)), vec,
                  pl.BlockSpec((d, d), lambda i: (0, 0)), pl.BlockSpec((dp, d), lambda i: (0, 0)), vec],
        out_specs=row,
        out_shape=jax.ShapeDtypeStruct((n, d), F32),
        compiler_params=_cparams(("parallel",)),
        name="ple",
    )(h1, moe, p2, gp, wg, wp, gf)


def kernel(x, p, g_mix, w_in, na_rpb, g_na_out, lam_q1, lam_k1, lam_q2, lam_k2, g_diff_out, w_out,
           g_moe, w_router, w_gate, w_up, w_down, g_ple, w_ple_gate, w_ple_proj, g_final):
    b, s, d = x.shape
    depth = w_in.shape[0]
    n = b * s
    na_w = NA_HEADS * HEAD_DIM
    cap = EC_CAPACITY_FACTOR * s // N_EXPERTS
    n_tile = s // MOE_TILE
    col_scale = np.ones((w_in.shape[-1],), np.float32)
    col_scale[:na_w] = HEAD_DIM ** -0.5 * LOG2E
    col_scale[3 * na_w:3 * na_w + DIFF_HEADS * 2 * HEAD_DIM] = HEAD_DIM ** -0.5 * LOG2E

    h = x.reshape(n, d)
    for i in range(depth):
        lam_init = 0.8 - 0.6 * math.exp(-0.3 * i)
        proj = _inproj(h, g_mix[i][None], (w_in[i] * col_scale).astype(BF16)).reshape(b, s, -1)
        o_na = _na_attention(proj, _na_bias_table(na_rpb[i]), g_na_out[i].reshape(1, na_w))
        lam_vecs = jnp.stack([lam_q1[i], lam_k1[i], lam_q2[i], lam_k2[i]]).astype(F32)
        o_df = _df_attention(proj, lam_vecs, g_diff_out[i][None], lam_init)
        h1, xn, aff = _outproj(o_na.reshape(n, -1), o_df.reshape(n, -1), h, w_out[i].astype(BF16),
                               g_moe[i][None], w_router[i].astype(BF16))
        aff_t = aff.reshape(b, s, N_EXPERTS).transpose(0, 2, 1)
        slot, gate, pref = _route(aff_t, cap)
        last = jnp.full((b, N_EXPERTS, 1), cap, F32)
        tstart = jnp.concatenate([pref[:, :, ::MOE_TILE], last], axis=-1).astype(jnp.int32).reshape(-1)
        cstart = jnp.concatenate([pref[:, :, ::COMB_TILE], last], axis=-1).astype(jnp.int32).reshape(-1)
        slot_rows = slot.reshape(b, N_EXPERTS, n_tile, 1, MOE_TILE)
        gate_rows = gate.reshape(b, N_EXPERTS, n_tile, 1, MOE_TILE)
        y = _moe_ffn(tstart, xn.reshape(b, s, d), slot_rows, gate_rows, w_gate[i].astype(BF16),
                     w_up[i].astype(BF16), w_down[i].astype(BF16), cap)
        comb_rows = slot.reshape(b, N_EXPERTS, s // COMB_TILE, 1, COMB_TILE).transpose(0, 2, 1, 3, 4)
        pack_start = _window_start(pref[:, :, ::COMB_TILE].astype(jnp.int32), cap, COMB_PACK)
        rel = slot - jnp.repeat(pack_start, COMB_TILE, axis=-1).astype(F32)
        rel = jnp.where(slot >= 0, rel, 255.0).transpose(0, 2, 1).astype(BF16)
        moe = _moe_combine(cstart, y, comb_rows, rel, s)
        h = _ple(h1, moe.reshape(n, d), p[i].reshape(n, -1), g_ple[i][None],
                 w_ple_gate[i].astype(BF16), w_ple_proj[i].astype(BF16), g_final[None],
                 last_layer=(i == depth - 1))
    return h.reshape(b, s, d)
```

```python
import functools
import math

import jax
import jax.numpy as jnp
import numpy as np
from jax import lax
from jax.experimental import pallas as pl
from jax.experimental.pallas import tpu as pltpu

F32 = jnp.float32
BF16 = jnp.bfloat16

EPS = 1e-6
GRID_W = 64
HEAD_DIM = 64
NA_HEADS = 8
NA_KH_MAX = 8
NA_KW = 16
DIFF_HEADS = 4
N_EXPERTS = 16
EC_CAPACITY_FACTOR = 2
LANES = 128
MOE_TILE = 256
MOE_WIN = 128
COMB_TILE = 128
COMB_PACK = 64
NEG_BIG = -1e30
LOG2E = math.log2(math.e)
VMEM_LIMIT = 56 * 1024 * 1024


def _cparams(sem):
    return pltpu.CompilerParams(dimension_semantics=sem, vmem_limit_bytes=VMEM_LIMIT)


def _rms(x, g):
    return x * lax.rsqrt(jnp.mean(x * x, axis=-1, keepdims=True) + EPS) * g


def _inproj_kernel(x_ref, g_ref, w_ref, o_ref, *, n_chunk):
    xn = _rms(x_ref[...], g_ref[...]).astype(BF16)
    cw = o_ref.shape[-1] // n_chunk
    for j in range(n_chunk):
        o_ref[:, j * cw:(j + 1) * cw] = jnp.dot(
            xn, w_ref[:, j * cw:(j + 1) * cw], preferred_element_type=F32).astype(BF16)


def _inproj(x2, g, w, tm=512):
    n, d = x2.shape
    nout = w.shape[1]
    return pl.pallas_call(
        functools.partial(_inproj_kernel, n_chunk=nout // 512),
        grid=(n // tm,),
        in_specs=[pl.BlockSpec((tm, d), lambda i: (i, 0)),
                  pl.BlockSpec((1, d), lambda i: (0, 0)),
                  pl.BlockSpec((d, nout), lambda i: (0, 0))],
        out_specs=pl.BlockSpec((tm, nout), lambda i: (i, 0)),
        out_shape=jax.ShapeDtypeStruct((n, nout), BF16),
        compiler_params=_cparams(("parallel",)),
        name="inproj",
    )(x2, g, w)


def _na_kernel(q_ref, k_ref, v_ref, bias_ref, g_ref, o_ref,
               s0_scr, s1_scr, p0_scr, p1_scr, l0_scr, l1_scr, *, rows, kh, rb):
    lane = lax.broadcasted_iota(jnp.int32, (GRID_W, LANES), 1)
    lo = lane < HEAD_DIM
    nkeys = kh * GRID_W
    n_pair = NA_HEADS // 2
    zero = jnp.zeros((GRID_W, LANES), BF16)

    def geometry(r):
        rs = jnp.clip(r - kh // 2, 0, rows - kh)
        return (pl.multiple_of(r * GRID_W, GRID_W), pl.multiple_of(rs * GRID_W, GRID_W),
                rs - r + NA_KH_MAX - 1)

    def scores(r, s_scr):
        q0, k0, _ = geometry(r)
        for j in range(n_pair):
            cs = slice(j * LANES, (j + 1) * LANES)
            qj = q_ref[0, pl.ds(q0, GRID_W), cs]
            q2 = jnp.concatenate([jnp.where(lo, qj, zero), jnp.where(lo, zero, qj)], axis=0)
            s_scr[j] = lax.dot_general(q2, k_ref[0, pl.ds(k0, nkeys), cs], (((1,), (1,)), ((), ())),
                                       preferred_element_type=F32)

    def softmax(r, s_scr, p_scr, l_scr):
        _, _, off = geometry(r)
        for j in range(n_pair):
            for sb in range(2 * GRID_W // rb):
                rws = slice(sb * rb, (sb + 1) * rb)
                head = 2 * j + (sb * rb) // GRID_W
                brow = slice((sb * rb) % GRID_W, (sb * rb) % GRID_W + rb)
                chunks = [s_scr[j, rws, i * LANES:(i + 1) * LANES] + bias_ref[head, off + 2 * i, brow, :]
                          for i in range(kh // 2)]
                m = chunks[0]
                for c in chunks[1:]:
                    m = jnp.maximum(m, c)
                m = jnp.max(m, axis=-1, keepdims=True)
                lsum = None
                for i, c in enumerate(chunks):
                    pi = jnp.exp2(c - m)
                    p_scr[j, rws, i * LANES:(i + 1) * LANES] = pi.astype(BF16)
                    lsum = pi if lsum is None else lsum + pi
                l_scr[j, rws, :] = lsum

    def values(r, p_scr, l_scr):
        q0, k0, _ = geometry(r)
        for j in range(n_pair):
            cs = slice(j * LANES, (j + 1) * LANES)
            of = jnp.dot(p_scr[j], v_ref[0, pl.ds(k0, nkeys), cs], preferred_element_type=F32)
            of = of / jnp.sum(l_scr[j], axis=-1, keepdims=True)
            o = jnp.where(lo, of[:GRID_W], of[GRID_W:])
            sq = o * o
            ms_a = jnp.sum(jnp.where(lo, sq, 0.0), axis=-1, keepdims=True) * (1.0 / HEAD_DIM)
            ms_b = jnp.sum(jnp.where(lo, 0.0, sq), axis=-1, keepdims=True) * (1.0 / HEAD_DIM)
            inv = jnp.where(lo, lax.rsqrt(ms_a + EPS), lax.rsqrt(ms_b + EPS))
            o_ref[0, pl.ds(q0, GRID_W), cs] = (o * inv * g_ref[:, cs]).astype(BF16)

    p1_scr[...] = jnp.zeros_like(p1_scr)
    l1_scr[...] = jnp.ones_like(l1_scr)
    scores(0, s0_scr)

    def pair_body(i, carry):
        r = 2 * i
        scores(r + 1, s1_scr)
        softmax(r, s0_scr, p0_scr, l0_scr)
        values(jnp.maximum(r - 1, 0), p1_scr, l1_scr)
        scores(jnp.minimum(r + 2, rows - 1), s0_scr)
        softmax(r + 1, s1_scr, p1_scr, l1_scr)
        values(r, p0_scr, l0_scr)
        return carry

    lax.fori_loop(0, rows // 2, pair_body, 0)
    values(rows - 1, p1_scr, l1_scr)


def _na_attention(proj, bias, g, rb=32):
    b, s, _ = proj.shape
    rows = s // GRID_W
    kh = min(NA_KH_MAX, rows)
    assert rows % 2 == 0 and kh % 2 == 0 and GRID_W % rb == 0
    w = NA_HEADS * HEAD_DIM
    n_pair = NA_HEADS // 2
    stacked = 2 * GRID_W
    return pl.pallas_call(
        functools.partial(_na_kernel, rows=rows, kh=kh, rb=rb),
        grid=(b,),
        in_specs=[pl.BlockSpec((1, s, w), lambda i: (i, 0, 0)),
                  pl.BlockSpec((1, s, w), lambda i: (i, 0, 1)),
                  pl.BlockSpec((1, s, w), lambda i: (i, 0, 2)),
                  pl.BlockSpec(bias.shape, lambda i: (0, 0, 0, 0)),
                  pl.BlockSpec((1, w), lambda i: (0, 0))],
        out_specs=pl.BlockSpec((1, s, w), lambda i: (i, 0, 0)),
        out_shape=jax.ShapeDtypeStruct((b, s, w), BF16),
        scratch_shapes=[pltpu.VMEM((n_pair, stacked, kh * GRID_W), F32),
                        pltpu.VMEM((n_pair, stacked, kh * GRID_W), F32),
                        pltpu.VMEM((n_pair, stacked, kh * GRID_W), BF16),
                        pltpu.VMEM((n_pair, stacked, kh * GRID_W), BF16),
                        pltpu.VMEM((n_pair, stacked, LANES), F32),
                        pltpu.VMEM((n_pair, stacked, LANES), F32)],
        compiler_params=_cparams(("parallel",)),
        name="na_attn",
    )(proj, proj, proj, bias, g)


def _na_bias_table(rpb):
    qcol = np.arange(GRID_W)[:, None]
    kcol = np.arange(GRID_W)[None, :]
    cstart = np.clip(qcol - NA_KW // 2, 0, GRID_W - NA_KW)
    ok = (kcol >= cstart) & (kcol < cstart + NA_KW)
    dc = np.clip(kcol - qcol + NA_KW - 1, 0, 2 * NA_KW - 2)
    pick = (dc[None] == np.arange(2 * NA_KW - 1)[:, None, None]) & ok[None]
    t = jnp.einsum('hdc,cqk->hdqk', rpb.astype(F32) * LOG2E, jnp.asarray(pick, F32),
                   precision=lax.Precision.HIGHEST)
    t = t + jnp.asarray(np.where(ok, 0.0, NEG_BIG), F32)
    return jnp.concatenate([t[:, :-1], t[:, 1:]], axis=-1)


def _split3(x):
    hi = x.astype(BF16).astype(F32)
    rest = x - hi
    mid = rest.astype(BF16).astype(F32)
    return hi, mid, rest - mid


def _df_kernel(q_ref, k_ref, v_ref, lam_ref, g_ref, o_ref,
               bias_scr, kaug_scr, q2_scr, s0_scr, s1_scr, p0_scr, p1_scr, a0_scr, a1_scr,
               m_scr, l_scr, acc_scr, *, tq, tk, rb, s_len, lam_init):
    h = pl.program_id(1)
    qi = pl.program_id(2)
    slope = jnp.exp2(-8.0 * (h + 1).astype(F32) / DIFF_HEADS) * LOG2E
    n_kb = s_len // tk
    n_par = tk // tq
    nt = (((1,), (1,)), ((), ()))

    @pl.when(qi == 0)
    def _():
        a = lax.broadcasted_iota(jnp.int32, (tq, tk), 0)
        b = lax.broadcasted_iota(jnp.int32, (tq, tk), 1)
        dist = (a - b).astype(F32)
        for t in range(n_par):
            bias_scr[t] = slope * jnp.abs(dist + float(t * tq))
        klane = lax.broadcasted_iota(jnp.int32, (tk, LANES), 1)
        hi, mid, low = _split3(slope * lax.broadcasted_iota(jnp.int32, (tk, LANES), 0).astype(F32))
        kaug = jnp.where(klane < 3, 1.0, jnp.where(klane == 3, hi, jnp.where(
            klane == 4, mid, jnp.where(klane == 5, low, 0.0))))
        kaug_scr[...] = kaug.astype(BF16)

    lane = lax.broadcasted_iota(jnp.int32, (tq, LANES), 1)
    lo = lane < HEAD_DIM
    q = q_ref[0]
    zero = jnp.zeros_like(q)
    qm = (jnp.where(lo, q, zero), jnp.where(lo, zero, q))
    q2d = jnp.concatenate(qm, axis=0)
    hi, mid, low = _split3(-slope * lax.broadcasted_iota(jnp.int32, (tq, LANES), 0).astype(F32))
    qaug = jnp.where(lane == 0, hi, jnp.where(lane == 1, mid, jnp.where(
        lane == 2, low, jnp.where(lane < 6, 1.0, 0.0))))
    for v, sign in enumerate((1.0, -1.0)):
        aug = (sign * qaug).astype(BF16)
        q2_scr[v] = jnp.concatenate([jnp.concatenate([qm[0], aug], axis=1),
                                     jnp.concatenate([qm[1], aug], axis=1)], axis=0)

    k_diag = jnp.right_shift(qi, int(math.log2(n_par)))
    par = jnp.bitwise_and(qi, n_par - 1)

    def block(r):
        kb = k_diag + r
        wrapped = kb >= n_kb
        kb = jnp.where(wrapped, kb - n_kb, kb)
        return pl.multiple_of(kb * tk, tk), wrapped

    def scores(r, s_scr):
        k0, wrapped = block(r)
        k_blk = k_ref[0, pl.ds(k0, tk), :]
        if r == 0:
            s_scr[...] = lax.dot_general(q2d, k_blk, nt, preferred_element_type=F32)
        else:
            s_scr[...] = lax.dot_general(q2_scr[jnp.where(wrapped, 0, 1)],
                                         jnp.concatenate([k_blk, kaug_scr[...]], axis=1), nt,
                                         preferred_element_type=F32)

    def softmax(r, s_scr, p_scr, a_scr):
        k0, wrapped = block(r)
        if r == 0:
            beta = 0.0
        else:
            cf = slope * (qi * tq - k0).astype(F32)
            beta = jnp.where(wrapped, -cf, cf)
        for sb in range(2 * tq // rb):
            rows = slice(sb * rb, (sb + 1) * rb)
            brow = slice((sb * rb) % tq, (sb * rb) % tq + rb)
            s = s_scr[rows, :]
            if r == 0:
                s = s - bias_scr[par, brow, :]
            m_old = m_scr[rows, :]
            m_new = jnp.maximum(m_old, jnp.max(s, axis=-1, keepdims=True) + beta)
            alpha = jnp.exp2(m_old - m_new)
            shift = m_new - beta
            psum = None
            for j in range(tk // LANES):
                cols = slice(j * LANES, (j + 1) * LANES)
                pj = jnp.exp2(s[:, cols] - shift)
                p_scr[rows, cols] = pj.astype(BF16)
                psum = pj if psum is None else psum + pj
            l_scr[rows, :] = alpha * l_scr[rows, :] + psum
            m_scr[rows, :] = m_new
            a_scr[rows, :] = alpha

    def values(r, p_scr, a_scr):
        k0, _ = block(r)
        acc_scr[...] = a_scr[...] * acc_scr[...] + jnp.dot(
            p_scr[...], v_ref[0, pl.ds(k0, tk), :], preferred_element_type=F32)

    m_scr[...] = jnp.full_like(m_scr, NEG_BIG)
    l_scr[...] = jnp.zeros_like(l_scr)
    acc_scr[...] = jnp.zeros_like(acc_scr)
    s_bufs, p_bufs, a_bufs = (s0_scr, s1_scr), (p0_scr, p1_scr), (a0_scr, a1_scr)
    scores(0, s0_scr)
    for r in range(n_kb):
        cur, oth = r % 2, 1 - r % 2
        if r + 1 < n_kb:
            scores(r + 1, s_bufs[oth])
        softmax(r, s_bufs[cur], p_bufs[cur], a_bufs[cur])
        if r >= 1:
            values(r - 1, p_bufs[oth], a_bufs[oth])
    values(n_kb - 1, p_bufs[(n_kb - 1) % 2], a_bufs[(n_kb - 1) % 2])
    o2 = acc_scr[...] / jnp.sum(l_scr[...], axis=-1, keepdims=True)
    lv = lam_ref[...]
    lam = (jnp.exp(jnp.sum(lv[0:1] * lv[1:2], axis=-1, keepdims=True))
           - jnp.exp(jnp.sum(lv[2:3] * lv[3:4], axis=-1, keepdims=True)) + lam_init)
    o = o2[:tq] - lam * o2[tq:]
    o_ref[0] = (_rms(o, g_ref[...]) * (1.0 - lam_init)).astype(BF16)


def _df_attention(proj, lam_vecs, g, lam_init, tq=256, tk=512, rb=32):
    b, s, _ = proj.shape
    tq = min(tq, s)
    tk = min(tk, s)
    assert tk % tq == 0
    qb, kb, vb = 12, 16, 20
    return pl.pallas_call(
        functools.partial(_df_kernel, tq=tq, tk=tk, rb=rb, s_len=s, lam_init=lam_init),
        grid=(b, DIFF_HEADS, s // tq),
        in_specs=[pl.BlockSpec((1, tq, LANES), lambda i, h, j: (i, j, qb + h)),
                  pl.BlockSpec((1, s, LANES), lambda i, h, j: (i, 0, kb + h)),
                  pl.BlockSpec((1, s, LANES), lambda i, h, j: (i, 0, vb + h)),
                  pl.BlockSpec((4, HEAD_DIM), lambda i, h, j: (0, 0)),
                  pl.BlockSpec((1, LANES), lambda i, h, j: (0, 0))],
        out_specs=pl.BlockSpec((1, tq, LANES), lambda i, h, j: (i, j, h)),
        out_shape=jax.ShapeDtypeStruct((b, s, DIFF_HEADS * LANES), BF16),
        scratch_shapes=[pltpu.VMEM((tk // tq, tq, tk), F32),
                        pltpu.VMEM((tk, LANES), BF16),
                        pltpu.VMEM((2, 2 * tq, 2 * LANES), BF16),
                        pltpu.VMEM((2 * tq, tk), F32),
                        pltpu.VMEM((2 * tq, tk), F32),
                        pltpu.VMEM((2 * tq, tk), BF16),
                        pltpu.VMEM((2 * tq, tk), BF16),
                        pltpu.VMEM((2 * tq, LANES), F32),
                        pltpu.VMEM((2 * tq, LANES), F32),
                        pltpu.VMEM((2 * tq, LANES), F32),
                        pltpu.VMEM((2 * tq, LANES), F32),
                        pltpu.VMEM((2 * tq, LANES), F32)],
        compiler_params=_cparams(("parallel", "parallel", "arbitrary")),
        name="df_attn",
    )(proj, proj, proj, lam_vecs, g)


def _outproj_kernel(ona_ref, odf_ref, x_ref, wo_ref, g_ref, wr_ref, h1_ref, xn_ref, aff_ref):
    half = ona_ref.shape[-1]
    acc = jnp.dot(ona_ref[...], wo_ref[:half, :], preferred_element_type=F32)
    acc = acc + jnp.dot(odf_ref[...], wo_ref[half:, :], preferred_element_type=F32)
    h1 = x_ref[...] + acc
    h1_ref[...] = h1
    xn = _rms(h1, g_ref[...]).astype(BF16)
    xn_ref[...] = xn
    logits = jnp.dot(xn, wr_ref[...], preferred_element_type=F32)
    z = jnp.exp(logits - jnp.max(logits, axis=-1, keepdims=True))
    aff_ref[...] = z / jnp.sum(z, axis=-1, keepdims=True)


def _outproj(o_na, o_df, x2, wo, g, wr, tm=1024):
    n, d = x2.shape
    half = o_na.shape[-1]
    e = wr.shape[-1]
    return pl.pallas_call(
        _outproj_kernel,
        grid=(n // tm,),
        in_specs=[pl.BlockSpec((tm, half), lambda i: (i, 0)),
                  pl.BlockSpec((tm, half), lambda i: (i, 0)),
                  pl.BlockSpec((tm, d), lambda i: (i, 0)),
                  pl.BlockSpec((2 * half, d), lambda i: (0, 0)),
                  pl.BlockSpec((1, d), lambda i: (0, 0)),
                  pl.BlockSpec((d, e), lambda i: (0, 0))],
        out_specs=[pl.BlockSpec((tm, d), lambda i: (i, 0)),
                   pl.BlockSpec((tm, d), lambda i: (i, 0)),
                   pl.BlockSpec((tm, e), lambda i: (i, 0))],
        out_shape=[jax.ShapeDtypeStruct((n, d), F32),
                   jax.ShapeDtypeStruct((n, d), BF16),
                   jax.ShapeDtypeStruct((n, e), F32)],
        compiler_params=_cparams(("parallel",)),
        name="outproj",
    )(o_na, o_df, x2, wo, g, wr)


def _excl_prefix(mask, upper):
    e, s = mask.shape
    mb = jnp.where(mask, 1.0, 0.0).astype(BF16)
    carry = jnp.zeros((e, 1), F32)
    outs = []
    for c in range(s // LANES):
        xc = mb[:, c * LANES:(c + 1) * LANES]
        outs.append(jnp.dot(xc, upper, preferred_element_type=F32) + carry)
        carry = carry + jnp.sum(xc.astype(F32), axis=-1, keepdims=True)
    return jnp.concatenate(outs, axis=1)


def _route_kernel(aff_ref, slot_ref, gate_ref, pref_ref, *, cap):
    a = aff_ref[0]
    e = a.shape[0]

    def count_ge(t):
        return jnp.sum(jnp.where(a >= t, 1, 0), axis=-1, keepdims=True)

    def bit_body(i, bits):
        trial = bits | jnp.left_shift(jnp.int32(1), 30 - i)
        return jnp.where(count_ge(pltpu.bitcast(trial, F32)) >= cap, trial, bits)

    thr = pltpu.bitcast(lax.fori_loop(0, 31, bit_body, jnp.zeros((e, 1), jnp.int32)), F32)
    step0 = thr
    for j in range(1, 21):
        trial = thr + step0 * (2.0 ** -(24 + j))
        thr = jnp.where(count_ge(trial) >= cap, trial, thr)
    gt = a > thr
    eq = a == thr
    need = (cap - jnp.sum(jnp.where(gt, 1, 0), axis=-1, keepdims=True)).astype(F32)
    r = lax.broadcasted_iota(jnp.int32, (LANES, LANES), 0)
    c = lax.broadcasted_iota(jnp.int32, (LANES, LANES), 1)
    upper = jnp.where(r < c, 1.0, 0.0).astype(BF16)
    sel = gt | (eq & (_excl_prefix(eq, upper) < need))
    pref = _excl_prefix(sel, upper)
    pref_ref[0] = pref
    slot_ref[0] = jnp.where(sel, pref, -1.0)
    gate_ref[0] = jnp.where(sel, a, 0.0)


def _route(aff_t, cap):
    b, e, s = aff_t.shape
    spec = pl.BlockSpec((1, e, s), lambda i: (i, 0, 0))
    shp = jax.ShapeDtypeStruct((b, e, s), F32)
    return pl.pallas_call(
        functools.partial(_route_kernel, cap=cap),
        grid=(b,),
        in_specs=[spec],
        out_specs=[spec, spec, spec],
        out_shape=[shp, shp, shp],
        compiler_params=_cparams(("parallel",)),
        name="route",
    )(aff_t)


def _window_range(ts_ref, base, t):
    s0 = ts_ref[base + t]
    s1 = ts_ref[base + t + 1]
    w_lo = jnp.right_shift(s0, int(math.log2(MOE_WIN)))
    w_hi = jnp.where(s1 > s0, jnp.right_shift(s1 - 1, int(math.log2(MOE_WIN))) + 1, w_lo)
    return w_lo, w_hi


def _moe_ffn_kernel(ts_ref, xn_ref, slot_ref, gate_ref, wg_ref, wu_ref, wd_ref, y_ref, xs_acc, gs_acc,
                    *, n_tile, n_fchunk):
    b = pl.program_id(0)
    e = pl.program_id(1)
    base = (b * pl.num_programs(1) + e) * (n_tile + 1)
    xs_acc[...] = jnp.zeros_like(xs_acc)
    gs_acc[...] = jnp.zeros_like(gs_acc)
    sub = lax.broadcasted_iota(jnp.int32, (MOE_WIN, MOE_TILE), 0).astype(F32)
    cap = xs_acc.shape[0]
    last_win = cap // MOE_WIN - 1

    def add_rows(r0, first_slot, srow, grow, xt):
        match = srow - first_slot.astype(F32) == sub
        onehot = jnp.where(match, 1.0, 0.0).astype(BF16)
        xs_acc[pl.ds(r0, MOE_WIN), :] += jnp.dot(onehot, xt, preferred_element_type=F32)
        gs_acc[pl.ds(r0, MOE_WIN), :] += jnp.sum(jnp.where(match, grow, 0.0), axis=-1, keepdims=True)

    def compact(w, srow, grow, xt):
        add_rows(pl.multiple_of(jnp.minimum(w, last_win) * MOE_WIN, MOE_WIN), w * MOE_WIN, srow, grow, xt)

    n_max = ts_ref[base + 1] - ts_ref[base]
    for t in range(1, n_tile):
        n_max = jnp.maximum(n_max, ts_ref[base + t + 1] - ts_ref[base + t])
    single = n_max <= MOE_WIN - 7

    @pl.when(single)
    def _():
        for t in range(n_tile):
            s0 = ts_ref[base + t]
            start = jnp.minimum(jnp.left_shift(jnp.right_shift(s0, 3), 3), cap - MOE_WIN)
            add_rows(pl.multiple_of(start, 8), start, slot_ref[0, 0, t], gate_ref[0, 0, t],
                     xn_ref[0, t * MOE_TILE:(t + 1) * MOE_TILE, :])

    @pl.when(jnp.logical_not(single))
    def _():
        def tile_body(t, carry):
            w_lo, w_hi = _window_range(ts_ref, base, t)
            srow = slot_ref[0, 0, t]
            grow = gate_ref[0, 0, t]
            xt = xn_ref[0, pl.ds(pl.multiple_of(t * MOE_TILE, MOE_TILE), MOE_TILE), :]

            def win_body(w, c2):
                compact(w, srow, grow, xt)
                return c2

            lax.fori_loop(w_lo + 2, w_hi, win_body, 0)
            return carry

        lax.fori_loop(0, n_tile, tile_body, 0)

        for t in range(n_tile):
            w_lo = jnp.right_shift(ts_ref[base + t], int(math.log2(MOE_WIN)))
            srow = slot_ref[0, 0, t]
            grow = gate_ref[0, 0, t]
            xt = xn_ref[0, t * MOE_TILE:(t + 1) * MOE_TILE, :]
            compact(w_lo, srow, grow, xt)
            compact(w_lo + 1, srow, grow, xt)

    xs = xs_acc[...].astype(BF16)
    f = wg_ref.shape[-1]
    fc = f // n_fchunk
    y = None
    for j in range(n_fchunk):
        fs = slice(j * fc, (j + 1) * fc)
        gt = jnp.dot(xs, wg_ref[0, :, fs].astype(BF16), preferred_element_type=F32)
        up = jnp.dot(xs, wu_ref[0, :, fs].astype(BF16), preferred_element_type=F32)
        hid = (gt * jax.nn.sigmoid(gt) * up).astype(BF16)
        part = jnp.dot(hid, wd_ref[0, fs, :].astype(BF16), preferred_element_type=F32)
        y = part if y is None else y + part
    y_ref[0, 0] = (y * gs_acc[...]).astype(BF16)


def _moe_ffn(tstart, xn, slot_rows, gate_rows, wg, wu, wd, cap):
    b, s, d = xn.shape
    e, _, f = wg.shape
    n_tile = s // MOE_TILE
    row_spec = pl.BlockSpec((1, 1, n_tile, 1, MOE_TILE), lambda i, j, ts: (i, j, 0, 0, 0))
    grid_spec = pltpu.PrefetchScalarGridSpec(
        num_scalar_prefetch=1,
        grid=(b, e),
        in_specs=[pl.BlockSpec((1, s, d), lambda i, j, ts: (i, 0, 0)),
                  row_spec, row_spec,
                  pl.BlockSpec((1, d, f), lambda i, j, ts: (j, 0, 0)),
                  pl.BlockSpec((1, d, f), lambda i, j, ts: (j, 0, 0)),
                  pl.BlockSpec((1, f, d), lambda i, j, ts: (j, 0, 0))],
        out_specs=pl.BlockSpec((1, 1, cap, d), lambda i, j, ts: (i, j, 0, 0)),
        scratch_shapes=[pltpu.VMEM((cap, d), F32), pltpu.VMEM((cap, 1), F32)],
    )
    return pl.pallas_call(
        functools.partial(_moe_ffn_kernel, n_tile=n_tile, n_fchunk=max(1, f // 256)),
        grid_spec=grid_spec,
        out_shape=jax.ShapeDtypeStruct((b, e, cap, d), BF16),
        compiler_params=_cparams(("parallel", "arbitrary")),
        name="moe_ffn",
    )(tstart, xn, slot_rows, gate_rows, wg, wu, wd)


def _window_start(s0, cap, win):
    return jnp.minimum(jnp.left_shift(jnp.right_shift(s0, 4), 4), cap - win)


def _moe_comb_kernel(ts_ref, y_ref, slot_ref, rel_ref, o_ref, ycat_scr, *, n_tile, win):
    b = pl.program_id(0)
    t = pl.program_id(1)
    n_exp, cap = y_ref.shape[1], y_ref.shape[2]

    def first_slot(e):
        return ts_ref[(b * n_exp + e) * (n_tile + 1) + t]

    n_max = ts_ref[b * n_exp * (n_tile + 1) + t + 1] - first_slot(0)
    for e in range(1, n_exp):
        n_max = jnp.maximum(n_max, ts_ref[(b * n_exp + e) * (n_tile + 1) + t + 1] - first_slot(e))
    packed = n_max <= COMB_PACK - 15

    @pl.when(packed)
    def _():
        for e in range(n_exp):
            start = pl.multiple_of(_window_start(first_slot(e), cap, COMB_PACK), 16)
            ycat_scr[e * COMB_PACK:(e + 1) * COMB_PACK, :] = y_ref[0, e, pl.ds(start, COMB_PACK), :]
        width = n_exp * COMB_PACK
        shift = int(math.log2(COMB_PACK))
        lane_e = jnp.right_shift(lax.broadcasted_iota(jnp.int32, (n_exp, width), 1), shift)
        spread = jnp.where(lane_e == lax.broadcasted_iota(jnp.int32, (n_exp, width), 0), 1.0, 0.0)
        rel = jnp.dot(rel_ref[0], spread.astype(BF16), preferred_element_type=F32)
        lane_j = jnp.bitwise_and(lax.broadcasted_iota(jnp.int32, (COMB_TILE, width), 1), COMB_PACK - 1)
        onehot = jnp.where(rel == lane_j.astype(F32), 1.0, 0.0).astype(BF16)
        o_ref[0] = jnp.dot(onehot, ycat_scr[...], preferred_element_type=F32)

    @pl.when(jnp.logical_not(packed))
    def _():
        lane = lax.broadcasted_iota(jnp.int32, (COMB_TILE, win), 1).astype(F32)
        acc = None
        for e in range(n_exp):
            start = pl.multiple_of(_window_start(first_slot(e), cap, win), 16)
            scol = jnp.broadcast_to(slot_ref[0, 0, e], (LANES, COMB_TILE)).T - start.astype(F32)
            scol = jnp.tile(scol, (1, win // LANES))
            onehot = jnp.where(scol == lane, 1.0, 0.0).astype(BF16)
            part = jnp.dot(onehot, y_ref[0, e, pl.ds(start, win), :], preferred_element_type=F32)
            acc = part if acc is None else acc + part
        o_ref[0] = acc


def _moe_combine(tstart, y, slot_rows, rel, s):
    b, e, cap, d = y.shape
    n_tile = s // COMB_TILE
    win = min(cap, 2 * COMB_TILE)
    assert win == cap or win >= COMB_TILE + 15
    grid_spec = pltpu.PrefetchScalarGridSpec(
        num_scalar_prefetch=1,
        grid=(b, n_tile),
        in_specs=[pl.BlockSpec((1, e, cap, d), lambda i, j, ts: (i, 0, 0, 0)),
                  pl.BlockSpec((1, 1, e, 1, COMB_TILE), lambda i, j, ts: (i, j, 0, 0, 0)),
                  pl.BlockSpec((1, COMB_TILE, e), lambda i, j, ts: (i, j, 0))],
        out_specs=pl.BlockSpec((1, COMB_TILE, d), lambda i, j, ts: (i, j, 0)),
        scratch_shapes=[pltpu.VMEM((e * COMB_PACK, d), BF16)],
    )
    return pl.pallas_call(
        functools.partial(_moe_comb_kernel, n_tile=n_tile, win=win),
        grid_spec=grid_spec,
        out_shape=jax.ShapeDtypeStruct((b, s, d), F32),
        compiler_params=_cparams(("parallel", "arbitrary")),
        name="moe_comb",
    )(tstart, y, slot_rows, rel)


def _ple_kernel(h1_ref, moe_ref, p_ref, gp_ref, wg_ref, wp_ref, gf_ref, o_ref, *, last_layer):
    h2 = h1_ref[...] + moe_ref[...]
    hn = _rms(h2, gp_ref[...]).astype(BF16)
    gate = jax.nn.sigmoid(jnp.dot(hn, wg_ref[...], preferred_element_type=F32))
    proj = jnp.dot(p_ref[...].astype(BF16), wp_ref[...], preferred_element_type=F32)
    h3 = h2 + gate * proj
    o_ref[...] = _rms(h3, gf_ref[...]) if last_layer else h3


def _ple(h1, moe, p2, gp, wg, wp, gf, last_layer, tm=1024):
    n, d = h1.shape
    dp = p2.shape[-1]
    row = pl.BlockSpec((tm, d), lambda i: (i, 0))
    vec = pl.BlockSpec((1, d), lambda i: (0, 0))
    return pl.pallas_call(
        functools.partial(_ple_kernel, last_layer=last_layer),
        grid=(n // tm,),
        in_specs=[row, row, pl.BlockSpec((tm, dp), lambda i: (i, 0)), vec,
                  pl.BlockSpec((d, d), lambda i: (0, 0)), pl.BlockSpec((dp, d), lambda i: (0, 0)), vec],
        out_specs=row,
        out_shape=jax.ShapeDtypeStruct((n, d), F32),
        compiler_params=_cparams(("parallel",)),
        name="ple",
    )(h1, moe, p2, gp, wg, wp, gf)


def kernel(x, p, g_mix, w_in, na_rpb, g_na_out, lam_q1, lam_k1, lam_q2, lam_k2, g_diff_out, w_out,
           g_moe, w_router, w_gate, w_up, w_down, g_ple, w_ple_gate, w_ple_proj, g_final):
    b, s, d = x.shape
    depth = w_in.shape[0]
    n = b * s
    na_w = NA_HEADS * HEAD_DIM
    cap = EC_CAPACITY_FACTOR * s // N_EXPERTS
    n_tile = s // MOE_TILE
    col_scale = np.ones((w_in.shape[-1],), np.float32)
    col_scale[:na_w] = HEAD_DIM ** -0.5 * LOG2E
    col_scale[3 * na_w:3 * na_w + DIFF_HEADS * 2 * HEAD_DIM] = HEAD_DIM ** -0.5 * LOG2E

    h = x.reshape(n, d)
    for i in range(depth):
        lam_init = 0.8 - 0.6 * math.exp(-0.3 * i)
        proj = _inproj(h, g_mix[i][None], (w_in[i] * col_scale).astype(BF16)).reshape(b, s, -1)
        o_na = _na_attention(proj, _na_bias_table(na_rpb[i]), g_na_out[i].reshape(1, na_w))
        lam_vecs = jnp.stack([lam_q1[i], lam_k1[i], lam_q2[i], lam_k2[i]]).astype(F32)
        o_df = _df_attention(proj, lam_vecs, g_diff_out[i][None], lam_init)
        h1, xn, aff = _outproj(o_na.reshape(n, -1), o_df.reshape(n, -1), h, w_out[i].astype(BF16),
                               g_moe[i][None], w_router[i].astype(BF16))
        aff_t = aff.reshape(b, s, N_EXPERTS).transpose(0, 2, 1)
        slot, gate, pref = _route(aff_t, cap)
        last = jnp.full((b, N_EXPERTS, 1), cap, F32)
        tstart = jnp.concatenate([pref[:, :, ::MOE_TILE], last], axis=-1).astype(jnp.int32).reshape(-1)
        cstart = jnp.concatenate([pref[:, :, ::COMB_TILE], last], axis=-1).astype(jnp.int32).reshape(-1)
        slot_rows = slot.reshape(b, N_EXPERTS, n_tile, 1, MOE_TILE)
        gate_rows = gate.reshape(b, N_EXPERTS, n_tile, 1, MOE_TILE)
        y = _moe_ffn(tstart, xn.reshape(b, s, d), slot_rows, gate_rows, w_gate[i], w_up[i], w_down[i], cap)
        comb_rows = slot.reshape(b, N_EXPERTS, s // COMB_TILE, 1, COMB_TILE).transpose(0, 2, 1, 3, 4)
        pack_start = _window_start(pref[:, :, ::COMB_TILE].astype(jnp.int32), cap, COMB_PACK)
        rel = slot - jnp.repeat(pack_start, COMB_TILE, axis=-1).astype(F32)
        rel = jnp.where(slot >= 0, rel, 255.0).transpose(0, 2, 1).astype(BF16)
        moe = _moe_combine(cstart, y, comb_rows, rel, s)
        h = _ple(h1, moe.reshape(n, d), p[i].reshape(n, -1), g_ple[i][None],
                 w_ple_gate[i].astype(BF16), w_ple_proj[i].astype(BF16), g_final[None],
                 last_layer=(i == depth - 1))
    return h.reshape(b, s, d)
```

```python
import functools
import math

import jax
import jax.numpy as jnp
import numpy as np
from jax import lax
from jax.experimental import pallas as pl
from jax.experimental.pallas import tpu as pltpu

F32 = jnp.float32
BF16 = jnp.bfloat16

EPS = 1e-6
GRID_W = 64
HEAD_DIM = 64
NA_HEADS = 8
NA_KH_MAX = 8
NA_KW = 16
DIFF_HEADS = 4
N_EXPERTS = 16
EC_CAPACITY_FACTOR = 2
LANES = 128
MOE_TILE = 256
MOE_WIN = 128
DF_TK = 512
COMB_TILE = 128
COMB_PACK = 64
NEG_BIG = -1e30
LOG2E = math.log2(math.e)
VMEM_LIMIT = 56 * 1024 * 1024


def _cparams(sem):
    return pltpu.CompilerParams(dimension_semantics=sem, vmem_limit_bytes=VMEM_LIMIT)


def _rms(x, g):
    return x * lax.rsqrt(jnp.mean(x * x, axis=-1, keepdims=True) + EPS) * g


def _inproj_kernel(x_ref, g_ref, w_ref, wvt_ref, o_ref, vt_ref, *, n_chunk):
    xn = _rms(x_ref[...], g_ref[...]).astype(BF16)
    cw = o_ref.shape[-1] // n_chunk
    for j in range(n_chunk):
        o_ref[:, j * cw:(j + 1) * cw] = jnp.dot(
            xn, w_ref[:, j * cw:(j + 1) * cw], preferred_element_type=F32).astype(BF16)
    vt_ref[0, 0] = lax.dot_general(wvt_ref[...], xn, (((1,), (1,)), ((), ())),
                                   preferred_element_type=F32).astype(BF16)


def _inproj(x2, g, w, wvt, seq, tm=DF_TK):
    n, d = x2.shape
    nout = w.shape[1]
    vw = wvt.shape[0]
    n_kb = seq // tm
    return pl.pallas_call(
        functools.partial(_inproj_kernel, n_chunk=nout // 512),
        grid=(n // tm,),
        in_specs=[pl.BlockSpec((tm, d), lambda i: (i, 0)),
                  pl.BlockSpec((1, d), lambda i: (0, 0)),
                  pl.BlockSpec((d, nout), lambda i: (0, 0)),
                  pl.BlockSpec((vw, d), lambda i: (0, 0))],
        out_specs=[pl.BlockSpec((tm, nout), lambda i: (i, 0)),
                   pl.BlockSpec((1, 1, vw, tm), lambda i: (i // n_kb, i % n_kb, 0, 0))],
        out_shape=[jax.ShapeDtypeStruct((n, nout), BF16),
                   jax.ShapeDtypeStruct((n // seq, n_kb, vw, tm), BF16)],
        compiler_params=_cparams(("parallel",)),
        name="inproj",
    )(x2, g, w, wvt)


def _na_kernel(q_ref, k_ref, v_ref, bias_ref, g_ref, o_ref,
               s0_scr, s1_scr, p0_scr, p1_scr, l0_scr, l1_scr, *, rows, kh, rb):
    lane = lax.broadcasted_iota(jnp.int32, (GRID_W, LANES), 1)
    lo = lane < HEAD_DIM
    nkeys = kh * GRID_W
    n_pair = NA_HEADS // 2
    zero = jnp.zeros((GRID_W, LANES), BF16)

    def geometry(r):
        rs = jnp.clip(r - kh // 2, 0, rows - kh)
        return (pl.multiple_of(r * GRID_W, GRID_W), pl.multiple_of(rs * GRID_W, GRID_W),
                rs - r + NA_KH_MAX - 1)

    def scores(r, s_scr):
        q0, k0, _ = geometry(r)
        for j in range(n_pair):
            cs = slice(j * LANES, (j + 1) * LANES)
            qj = q_ref[0, pl.ds(q0, GRID_W), cs]
            q2 = jnp.concatenate([jnp.where(lo, qj, zero), jnp.where(lo, zero, qj)], axis=0)
            s_scr[j] = lax.dot_general(q2, k_ref[0, pl.ds(k0, nkeys), cs], (((1,), (1,)), ((), ())),
                                       preferred_element_type=F32)

    def softmax(r, s_scr, p_scr, l_scr):
        _, _, off = geometry(r)
        for j in range(n_pair):
            for sb in range(2 * GRID_W // rb):
                rws = slice(sb * rb, (sb + 1) * rb)
                head = 2 * j + (sb * rb) // GRID_W
                brow = slice((sb * rb) % GRID_W, (sb * rb) % GRID_W + rb)
                chunks = [s_scr[j, rws, i * LANES:(i + 1) * LANES] + bias_ref[head, off + 2 * i, brow, :]
                          for i in range(kh // 2)]
                m = chunks[0]
                for c in chunks[1:]:
                    m = jnp.maximum(m, c)
                m = jnp.max(m, axis=-1, keepdims=True)
                lsum = None
                for i, c in enumerate(chunks):
                    pi = jnp.exp2(c - m)
                    p_scr[j, rws, i * LANES:(i + 1) * LANES] = pi.astype(BF16)
                    lsum = pi if lsum is None else lsum + pi
                l_scr[j, rws, :] = lsum

    def values(r, p_scr, l_scr):
        q0, k0, _ = geometry(r)
        for j in range(n_pair):
            cs = slice(j * LANES, (j + 1) * LANES)
            of = jnp.dot(p_scr[j], v_ref[0, pl.ds(k0, nkeys), cs], preferred_element_type=F32)
            of = of / jnp.sum(l_scr[j], axis=-1, keepdims=True)
            o = jnp.where(lo, of[:GRID_W], of[GRID_W:])
            sq = o * o
            ms_a = jnp.sum(jnp.where(lo, sq, 0.0), axis=-1, keepdims=True) * (1.0 / HEAD_DIM)
            ms_b = jnp.sum(jnp.where(lo, 0.0, sq), axis=-1, keepdims=True) * (1.0 / HEAD_DIM)
            inv = jnp.where(lo, lax.rsqrt(ms_a + EPS), lax.rsqrt(ms_b + EPS))
            o_ref[0, pl.ds(q0, GRID_W), cs] = (o * inv * g_ref[:, cs]).astype(BF16)

    p1_scr[...] = jnp.zeros_like(p1_scr)
    l1_scr[...] = jnp.ones_like(l1_scr)
    scores(0, s0_scr)

    def pair_body(i, carry):
        r = 2 * i
        scores(r + 1, s1_scr)
        softmax(r, s0_scr, p0_scr, l0_scr)
        values(jnp.maximum(r - 1, 0), p1_scr, l1_scr)
        scores(jnp.minimum(r + 2, rows - 1), s0_scr)
        softmax(r + 1, s1_scr, p1_scr, l1_scr)
        values(r, p0_scr, l0_scr)
        return carry

    lax.fori_loop(0, rows // 2, pair_body, 0)
    values(rows - 1, p1_scr, l1_scr)


def _na_attention(proj, bias, g, rb=32):
    b, s, _ = proj.shape
    rows = s // GRID_W
    kh = min(NA_KH_MAX, rows)
    assert rows % 2 == 0 and kh % 2 == 0 and GRID_W % rb == 0
    w = NA_HEADS * HEAD_DIM
    n_pair = NA_HEADS // 2
    stacked = 2 * GRID_W
    return pl.pallas_call(
        functools.partial(_na_kernel, rows=rows, kh=kh, rb=rb),
        grid=(b,),
        in_specs=[pl.BlockSpec((1, s, w), lambda i: (i, 0, 0)),
                  pl.BlockSpec((1, s, w), lambda i: (i, 0, 1)),
                  pl.BlockSpec((1, s, w), lambda i: (i, 0, 2)),
                  pl.BlockSpec(bias.shape, lambda i: (0, 0, 0, 0)),
                  pl.BlockSpec((1, w), lambda i: (0, 0))],
        out_specs=pl.BlockSpec((1, s, w), lambda i: (i, 0, 0)),
        out_shape=jax.ShapeDtypeStruct((b, s, w), BF16),
        scratch_shapes=[pltpu.VMEM((n_pair, stacked, kh * GRID_W), F32),
                        pltpu.VMEM((n_pair, stacked, kh * GRID_W), F32),
                        pltpu.VMEM((n_pair, stacked, kh * GRID_W), BF16),
                        pltpu.VMEM((n_pair, stacked, kh * GRID_W), BF16),
                        pltpu.VMEM((n_pair, stacked, LANES), F32),
                        pltpu.VMEM((n_pair, stacked, LANES), F32)],
        compiler_params=_cparams(("parallel",)),
        name="na_attn",
    )(proj, proj, proj, bias, g)


def _na_bias_table(rpb):
    qcol = np.arange(GRID_W)[:, None]
    kcol = np.arange(GRID_W)[None, :]
    cstart = np.clip(qcol - NA_KW // 2, 0, GRID_W - NA_KW)
    ok = (kcol >= cstart) & (kcol < cstart + NA_KW)
    dc = np.clip(kcol - qcol + NA_KW - 1, 0, 2 * NA_KW - 2)
    pick = (dc[None] == np.arange(2 * NA_KW - 1)[:, None, None]) & ok[None]
    t = jnp.einsum('hdc,cqk->hdqk', rpb.astype(F32) * LOG2E, jnp.asarray(pick, F32),
                   precision=lax.Precision.HIGHEST)
    t = t + jnp.asarray(np.where(ok, 0.0, NEG_BIG), F32)
    return jnp.concatenate([t[:, :-1], t[:, 1:]], axis=-1)


def _split3(x):
    hi = x.astype(BF16).astype(F32)
    rest = x - hi
    mid = rest.astype(BF16).astype(F32)
    return hi, mid, rest - mid


def _df_kernel(q_ref, k_ref, vt_ref, lam_ref, g_ref, o_ref,
               bias_scr, kaug_scr, q2_scr, s0_scr, s1_scr, p0_scr, p1_scr, a0_scr, a1_scr,
               m_scr, l_scr, acc_scr, *, tq, tk, kc, s_len, lam_init):
    h = pl.program_id(1)
    qi = pl.program_id(2)
    slope = jnp.exp2(-8.0 * (h + 1).astype(F32) / DIFF_HEADS) * LOG2E
    n_kb = s_len // tk
    n_par = tk // tq
    nt = (((1,), (1,)), ((), ()))

    @pl.when(qi == 0)
    def _():
        a = jnp.bitwise_and(lax.broadcasted_iota(jnp.int32, (tk, 2 * tq), 1), tq - 1)
        b = lax.broadcasted_iota(jnp.int32, (tk, 2 * tq), 0)
        dist = (a - b).astype(F32)
        for t in range(n_par):
            bias_scr[t] = slope * jnp.abs(dist + float(t * tq))
        klane = lax.broadcasted_iota(jnp.int32, (tk, LANES), 1)
        hi, mid, low = _split3(slope * lax.broadcasted_iota(jnp.int32, (tk, LANES), 0).astype(F32))
        kaug = jnp.where(klane < 3, 1.0, jnp.where(klane == 3, hi, jnp.where(
            klane == 4, mid, jnp.where(klane == 5, low, 0.0))))
        kaug_scr[...] = kaug.astype(BF16)

    lane = lax.broadcasted_iota(jnp.int32, (tq, LANES), 1)
    lo = lane < HEAD_DIM
    q = q_ref[0]
    zero = jnp.zeros_like(q)
    qm = (jnp.where(lo, q, zero), jnp.where(lo, zero, q))
    q2d = jnp.concatenate(qm, axis=0)
    hi, mid, low = _split3(-slope * lax.broadcasted_iota(jnp.int32, (tq, LANES), 0).astype(F32))
    qaug = jnp.where(lane == 0, hi, jnp.where(lane == 1, mid, jnp.where(
        lane == 2, low, jnp.where(lane < 6, 1.0, 0.0))))
    for v, sign in enumerate((1.0, -1.0)):
        aug = (sign * qaug).astype(BF16)
        q2_scr[v] = jnp.concatenate([jnp.concatenate([qm[0], aug], axis=1),
                                     jnp.concatenate([qm[1], aug], axis=1)], axis=0)

    k_diag = jnp.right_shift(qi, int(math.log2(n_par)))
    par = jnp.bitwise_and(qi, n_par - 1)

    def block(r):
        kb = k_diag + r
        wrapped = kb >= n_kb
        kb = jnp.where(wrapped, kb - n_kb, kb)
        return kb, pl.multiple_of(kb * tk, tk), wrapped

    def scores(r, s_scr):
        _, k0, wrapped = block(r)
        k_blk = k_ref[0, pl.ds(k0, tk), :]
        if r == 0:
            s_scr[...] = lax.dot_general(k_blk, q2d, nt, preferred_element_type=F32)
        else:
            s_scr[...] = lax.dot_general(jnp.concatenate([k_blk, kaug_scr[...]], axis=1),
                                         q2_scr[jnp.where(wrapped, 0, 1)], nt,
                                         preferred_element_type=F32)

    def fold8(x, op):
        out = x[0:8]
        for g in range(1, x.shape[0] // 8):
            out = op(out, x[8 * g:8 * (g + 1)])
        return out

    def softmax(r, s_scr, p_scr, a_scr):
        _, k0, wrapped = block(r)
        if r == 0:
            beta = 0.0
        else:
            cf = slope * (qi * tq - k0).astype(F32)
            beta = jnp.where(wrapped, -cf, cf)
        mx = None
        for c in range(tk // kc):
            rows = slice(c * kc, (c + 1) * kc)
            blk = s_scr[rows, :]
            if r == 0:
                blk = blk - bias_scr[par, rows, :]
                s_scr[rows, :] = blk
            part = fold8(blk, jnp.maximum)
            mx = part if mx is None else jnp.maximum(mx, part)
        m_old = m_scr[...]
        m_new = jnp.maximum(m_old, jnp.max(mx, axis=0, keepdims=True) + beta)
        alpha = jnp.exp2(m_old - m_new)
        shift = m_new - beta
        psum = None
        for c in range(tk // kc):
            rows = slice(c * kc, (c + 1) * kc)
            p = jnp.exp2(s_scr[rows, :] - shift)
            p_scr[rows, :] = p.astype(BF16)
            part = fold8(p, jnp.add)
            psum = part if psum is None else psum + part
        l_scr[...] = alpha * l_scr[...] + psum
        m_scr[...] = m_new
        a_scr[...] = alpha

    def values(r, p_scr, a_scr):
        kb, _, _ = block(r)
        acc_scr[...] = a_scr[...] * acc_scr[...] + jnp.dot(
            vt_ref[0, kb], p_scr[...], preferred_element_type=F32)

    m_scr[...] = jnp.full_like(m_scr, NEG_BIG)
    l_scr[...] = jnp.zeros_like(l_scr)
    acc_scr[...] = jnp.zeros_like(acc_scr)
    s_bufs, p_bufs, a_bufs = (s0_scr, s1_scr), (p0_scr, p1_scr), (a0_scr, a1_scr)
    scores(0, s0_scr)
    for r in range(n_kb):
        cur, oth = r % 2, 1 - r % 2
        if r + 1 < n_kb:
            scores(r + 1, s_bufs[oth])
        softmax(r, s_bufs[cur], p_bufs[cur], a_bufs[cur])
        if r >= 1:
            values(r - 1, p_bufs[oth], a_bufs[oth])
    values(n_kb - 1, p_bufs[(n_kb - 1) % 2], a_bufs[(n_kb - 1) % 2])
    o2t = acc_scr[...] / jnp.sum(l_scr[...], axis=0, keepdims=True)
    lv = lam_ref[...]
    lam = (jnp.exp(jnp.sum(lv[0:1] * lv[1:2], axis=-1, keepdims=True))
           - jnp.exp(jnp.sum(lv[2:3] * lv[3:4], axis=-1, keepdims=True)) + lam_init)
    o = (o2t[:, :tq] - lam * o2t[:, tq:]).T
    o_ref[0] = (_rms(o, g_ref[...]) * (1.0 - lam_init)).astype(BF16)


def _df_attention(proj, vt, lam_vecs, g, lam_init, tq=512, tk=DF_TK, kc=128):
    b, s, _ = proj.shape
    tq = min(tq, s)
    assert tk % tq == 0 and s % tk == 0 and tk % kc == 0
    n_kb = s // tk
    qb, kb = 12, 16
    return pl.pallas_call(
        functools.partial(_df_kernel, tq=tq, tk=tk, kc=kc, s_len=s, lam_init=lam_init),
        grid=(b, DIFF_HEADS, s // tq),
        in_specs=[pl.BlockSpec((1, tq, LANES), lambda i, h, j: (i, j, qb + h)),
                  pl.BlockSpec((1, s, LANES), lambda i, h, j: (i, 0, kb + h)),
                  pl.BlockSpec((1, n_kb, LANES, tk), lambda i, h, j: (i, 0, h, 0)),
                  pl.BlockSpec((4, HEAD_DIM), lambda i, h, j: (0, 0)),
                  pl.BlockSpec((1, LANES), lambda i, h, j: (0, 0))],
        out_specs=pl.BlockSpec((1, tq, LANES), lambda i, h, j: (i, j, h)),
        out_shape=jax.ShapeDtypeStruct((b, s, DIFF_HEADS * LANES), BF16),
        scratch_shapes=[pltpu.VMEM((tk // tq, tk, 2 * tq), F32),
                        pltpu.VMEM((tk, LANES), BF16),
                        pltpu.VMEM((2, 2 * tq, 2 * LANES), BF16),
                        pltpu.VMEM((tk, 2 * tq), F32),
                        pltpu.VMEM((tk, 2 * tq), F32),
                        pltpu.VMEM((tk, 2 * tq), BF16),
                        pltpu.VMEM((tk, 2 * tq), BF16),
                        pltpu.VMEM((1, 2 * tq), F32),
                        pltpu.VMEM((1, 2 * tq), F32),
                        pltpu.VMEM((1, 2 * tq), F32),
                        pltpu.VMEM((8, 2 * tq), F32),
                        pltpu.VMEM((LANES, 2 * tq), F32)],
        compiler_params=_cparams(("parallel", "parallel", "arbitrary")),
        name="df_attn",
    )(proj, proj, vt, lam_vecs, g)


def _outproj_kernel(ona_ref, odf_ref, x_ref, wo_ref, g_ref, wr_ref, h1_ref, xn_ref, aff_ref):
    half = ona_ref.shape[-1]
    acc = jnp.dot(ona_ref[...], wo_ref[:half, :], preferred_element_type=F32)
    acc = acc + jnp.dot(odf_ref[...], wo_ref[half:, :], preferred_element_type=F32)
    h1 = x_ref[...] + acc
    h1_ref[...] = h1
    xn = _rms(h1, g_ref[...]).astype(BF16)
    xn_ref[...] = xn
    logits = jnp.dot(xn, wr_ref[...], preferred_element_type=F32)
    z = jnp.exp(logits - jnp.max(logits, axis=-1, keepdims=True))
    aff_ref[...] = z / jnp.sum(z, axis=-1, keepdims=True)


def _outproj(o_na, o_df, x2, wo, g, wr, tm=1024):
    n, d = x2.shape
    half = o_na.shape[-1]
    e = wr.shape[-1]
    return pl.pallas_call(
        _outproj_kernel,
        grid=(n // tm,),
        in_specs=[pl.BlockSpec((tm, half), lambda i: (i, 0)),
                  pl.BlockSpec((tm, half), lambda i: (i, 0)),
                  pl.BlockSpec((tm, d), lambda i: (i, 0)),
                  pl.BlockSpec((2 * half, d), lambda i: (0, 0)),
                  pl.BlockSpec((1, d), lambda i: (0, 0)),
                  pl.BlockSpec((d, e), lambda i: (0, 0))],
        out_specs=[pl.BlockSpec((tm, d), lambda i: (i, 0)),
                   pl.BlockSpec((tm, d), lambda i: (i, 0)),
                   pl.BlockSpec((tm, e), lambda i: (i, 0))],
        out_shape=[jax.ShapeDtypeStruct((n, d), F32),
                   jax.ShapeDtypeStruct((n, d), BF16),
                   jax.ShapeDtypeStruct((n, e), F32)],
        compiler_params=_cparams(("parallel",)),
        name="outproj",
    )(o_na, o_df, x2, wo, g, wr)


def _excl_prefix(mask, upper):
    e, s = mask.shape
    mb = jnp.where(mask, 1.0, 0.0).astype(BF16)
    carry = jnp.zeros((e, 1), F32)
    outs = []
    for c in range(s // LANES):
        xc = mb[:, c * LANES:(c + 1) * LANES]
        outs.append(jnp.dot(xc, upper, preferred_element_type=F32) + carry)
        carry = carry + jnp.sum(xc.astype(F32), axis=-1, keepdims=True)
    return jnp.concatenate(outs, axis=1)


def _route_kernel(aff_ref, slot_ref, gate_ref, pref_ref, *, cap):
    a = aff_ref[0]
    e = a.shape[0]

    def count_ge(t):
        return jnp.sum(jnp.where(a >= t, 1, 0), axis=-1, keepdims=True)

    def bit_body(i, bits):
        trial = bits | jnp.left_shift(jnp.int32(1), 30 - i)
        return jnp.where(count_ge(pltpu.bitcast(trial, F32)) >= cap, trial, bits)

    thr = pltpu.bitcast(lax.fori_loop(0, 31, bit_body, jnp.zeros((e, 1), jnp.int32)), F32)
    step0 = thr
    for j in range(1, 21):
        trial = thr + step0 * (2.0 ** -(24 + j))
        thr = jnp.where(count_ge(trial) >= cap, trial, thr)
    gt = a > thr
    eq = a == thr
    need = (cap - jnp.sum(jnp.where(gt, 1, 0), axis=-1, keepdims=True)).astype(F32)
    r = lax.broadcasted_iota(jnp.int32, (LANES, LANES), 0)
    c = lax.broadcasted_iota(jnp.int32, (LANES, LANES), 1)
    upper = jnp.where(r < c, 1.0, 0.0).astype(BF16)
    sel = gt | (eq & (_excl_prefix(eq, upper) < need))
    pref = _excl_prefix(sel, upper)
    pref_ref[0] = pref
    slot_ref[0] = jnp.where(sel, pref, -1.0)
    gate_ref[0] = jnp.where(sel, a, 0.0)


def _route(aff_t, cap):
    b, e, s = aff_t.shape
    spec = pl.BlockSpec((1, e, s), lambda i: (i, 0, 0))
    shp = jax.ShapeDtypeStruct((b, e, s), F32)
    return pl.pallas_call(
        functools.partial(_route_kernel, cap=cap),
        grid=(b,),
        in_specs=[spec],
        out_specs=[spec, spec, spec],
        out_shape=[shp, shp, shp],
        compiler_params=_cparams(("parallel",)),
        name="route",
    )(aff_t)


def _window_range(ts_ref, base, t):
    s0 = ts_ref[base + t]
    s1 = ts_ref[base + t + 1]
    w_lo = jnp.right_shift(s0, int(math.log2(MOE_WIN)))
    w_hi = jnp.where(s1 > s0, jnp.right_shift(s1 - 1, int(math.log2(MOE_WIN))) + 1, w_lo)
    return w_lo, w_hi


def _moe_ffn_kernel(ts_ref, xn_ref, slot_ref, gate_ref, wg_ref, wu_ref, wd_ref, y_ref, xs_acc, gs_acc,
                    *, n_tile, n_fchunk):
    b = pl.program_id(0)
    e = pl.program_id(1)
    base = (b * pl.num_programs(1) + e) * (n_tile + 1)
    xs_acc[...] = jnp.zeros_like(xs_acc)
    gs_acc[...] = jnp.zeros_like(gs_acc)
    sub = lax.broadcasted_iota(jnp.int32, (MOE_WIN, MOE_TILE), 0).astype(F32)
    cap = xs_acc.shape[0]
    last_win = cap // MOE_WIN - 1

    def add_rows(r0, first_slot, srow, grow, xt):
        match = srow - first_slot.astype(F32) == sub
        onehot = jnp.where(match, 1.0, 0.0).astype(BF16)
        xs_acc[pl.ds(r0, MOE_WIN), :] += jnp.dot(onehot, xt, preferred_element_type=F32)
        gs_acc[pl.ds(r0, MOE_WIN), :] += jnp.sum(jnp.where(match, grow, 0.0), axis=-1, keepdims=True)

    def compact(w, srow, grow, xt):
        add_rows(pl.multiple_of(jnp.minimum(w, last_win) * MOE_WIN, MOE_WIN), w * MOE_WIN, srow, grow, xt)

    n_max = ts_ref[base + 1] - ts_ref[base]
    for t in range(1, n_tile):
        n_max = jnp.maximum(n_max, ts_ref[base + t + 1] - ts_ref[base + t])
    single = n_max <= MOE_WIN - 7

    @pl.when(single)
    def _():
        for t in range(n_tile):
            s0 = ts_ref[base + t]
            start = jnp.minimum(jnp.left_shift(jnp.right_shift(s0, 3), 3), cap - MOE_WIN)
            add_rows(pl.multiple_of(start, 8), start, slot_ref[0, 0, t], gate_ref[0, 0, t],
                     xn_ref[0, t * MOE_TILE:(t + 1) * MOE_TILE, :])

    @pl.when(jnp.logical_not(single))
    def _():
        def tile_body(t, carry):
            w_lo, w_hi = _window_range(ts_ref, base, t)
            srow = slot_ref[0, 0, t]
            grow = gate_ref[0, 0, t]
            xt = xn_ref[0, pl.ds(pl.multiple_of(t * MOE_TILE, MOE_TILE), MOE_TILE), :]

            def win_body(w, c2):
                compact(w, srow, grow, xt)
                return c2

            lax.fori_loop(w_lo + 2, w_hi, win_body, 0)
            return carry

        lax.fori_loop(0, n_tile, tile_body, 0)

        for t in range(n_tile):
            w_lo = jnp.right_shift(ts_ref[base + t], int(math.log2(MOE_WIN)))
            srow = slot_ref[0, 0, t]
            grow = gate_ref[0, 0, t]
            xt = xn_ref[0, t * MOE_TILE:(t + 1) * MOE_TILE, :]
            compact(w_lo, srow, grow, xt)
            compact(w_lo + 1, srow, grow, xt)

    xs = xs_acc[...].astype(BF16)
    f = wg_ref.shape[-1]
    fc = f // n_fchunk
    y = None
    for j in range(n_fchunk):
        fs = slice(j * fc, (j + 1) * fc)
        gt = jnp.dot(xs, wg_ref[0, :, fs].astype(BF16), preferred_element_type=F32)
        up = jnp.dot(xs, wu_ref[0, :, fs].astype(BF16), preferred_element_type=F32)
        hid = (gt * jax.nn.sigmoid(gt) * up).astype(BF16)
        part = jnp.dot(hid, wd_ref[0, fs, :].astype(BF16), preferred_element_type=F32)
        y = part if y is None else y + part
    y_ref[0, 0] = (y * gs_acc[...]).astype(BF16)


def _moe_ffn(tstart, xn, slot_rows, gate_rows, wg, wu, wd, cap):
    b, s, d = xn.shape
    e, _, f = wg.shape
    n_tile = s // MOE_TILE
    row_spec = pl.BlockSpec((1, 1, n_tile, 1, MOE_TILE), lambda i, j, ts: (i, j, 0, 0, 0))
    grid_spec = pltpu.PrefetchScalarGridSpec(
        num_scalar_prefetch=1,
        grid=(b, e),
        in_specs=[pl.BlockSpec((1, s, d), lambda i, j, ts: (i, 0, 0)),
                  row_spec, row_spec,
                  pl.BlockSpec((1, d, f), lambda i, j, ts: (j, 0, 0)),
                  pl.BlockSpec((1, d, f), lambda i, j, ts: (j, 0, 0)),
                  pl.BlockSpec((1, f, d), lambda i, j, ts: (j, 0, 0))],
        out_specs=pl.BlockSpec((1, 1, cap, d), lambda i, j, ts: (i, j, 0, 0)),
        scratch_shapes=[pltpu.VMEM((cap, d), F32), pltpu.VMEM((cap, 1), F32)],
    )
    return pl.pallas_call(
        functools.partial(_moe_ffn_kernel, n_tile=n_tile, n_fchunk=max(1, f // 256)),
        grid_spec=grid_spec,
        out_shape=jax.ShapeDtypeStruct((b, e, cap, d), BF16),
        compiler_params=_cparams(("parallel", "arbitrary")),
        name="moe_ffn",
    )(tstart, xn, slot_rows, gate_rows, wg, wu, wd)


def _window_start(s0, cap, win):
    return jnp.minimum(jnp.left_shift(jnp.right_shift(s0, 4), 4), cap - win)


def _moe_comb_kernel(ts_ref, y_ref, slot_ref, rel_ref, o_ref, ycat_scr, *, n_tile, win):
    b = pl.program_id(0)
    t = pl.program_id(1)
    n_exp, cap = y_ref.shape[1], y_ref.shape[2]

    def first_slot(e):
        return ts_ref[(b * n_exp + e) * (n_tile + 1) + t]

    n_max = ts_ref[b * n_exp * (n_tile + 1) + t + 1] - first_slot(0)
    for e in range(1, n_exp):
        n_max = jnp.maximum(n_max, ts_ref[(b * n_exp + e) * (n_tile + 1) + t + 1] - first_slot(e))
    packed = n_max <= COMB_PACK - 15

    @pl.when(packed)
    def _():
        for e in range(n_exp):
            start = pl.multiple_of(_window_start(first_slot(e), cap, COMB_PACK), 16)
            ycat_scr[e * COMB_PACK:(e + 1) * COMB_PACK, :] = y_ref[0, e, pl.ds(start, COMB_PACK), :]
        width = n_exp * COMB_PACK
        shift = int(math.log2(COMB_PACK))
        lane_e = jnp.right_shift(lax.broadcasted_iota(jnp.int32, (n_exp, width), 1), shift)
        spread = jnp.where(lane_e == lax.broadcasted_iota(jnp.int32, (n_exp, width), 0), 1.0, 0.0)
        rel = jnp.dot(rel_ref[0], spread.astype(BF16), preferred_element_type=F32)
        lane_j = jnp.bitwise_and(lax.broadcasted_iota(jnp.int32, (COMB_TILE, width), 1), COMB_PACK - 1)
        onehot = jnp.where(rel == lane_j.astype(F32), 1.0, 0.0).astype(BF16)
        o_ref[0] = jnp.dot(onehot, ycat_scr[...], preferred_element_type=F32)

    @pl.when(jnp.logical_not(packed))
    def _():
        lane = lax.broadcasted_iota(jnp.int32, (COMB_TILE, win), 1).astype(F32)
        acc = None
        for e in range(n_exp):
            start = pl.multiple_of(_window_start(first_slot(e), cap, win), 16)
            scol = jnp.broadcast_to(slot_ref[0, 0, e], (LANES, COMB_TILE)).T - start.astype(F32)
            scol = jnp.tile(scol, (1, win // LANES))
            onehot = jnp.where(scol == lane, 1.0, 0.0).astype(BF16)
            part = jnp.dot(onehot, y_ref[0, e, pl.ds(start, win), :], preferred_element_type=F32)
            acc = part if acc is None else acc + part
        o_ref[0] = acc


def _moe_combine(tstart, y, slot_rows, rel, s):
    b, e, cap, d = y.shape
    n_tile = s // COMB_TILE
    win = min(cap, 2 * COMB_TILE)
    assert win == cap or win >= COMB_TILE + 15
    grid_spec = pltpu.PrefetchScalarGridSpec(
        num_scalar_prefetch=1,
        grid=(b, n_tile),
        in_specs=[pl.BlockSpec((1, e, cap, d), lambda i, j, ts: (i, 0, 0, 0)),
                  pl.BlockSpec((1, 1, e, 1, COMB_TILE), lambda i, j, ts: (i, j, 0, 0, 0)),
                  pl.BlockSpec((1, COMB_TILE, e), lambda i, j, ts: (i, j, 0))],
        out_specs=pl.BlockSpec((1, COMB_TILE, d), lambda i, j, ts: (i, j, 0)),
        scratch_shapes=[pltpu.VMEM((e * COMB_PACK, d), BF16)],
    )
    return pl.pallas_call(
        functools.partial(_moe_comb_kernel, n_tile=n_tile, win=win),
        grid_spec=grid_spec,
        out_shape=jax.ShapeDtypeStruct((b, s, d), F32),
        compiler_params=_cparams(("parallel", "arbitrary")),
        name="moe_comb",
    )(tstart, y, slot_rows, rel)


def _ple_kernel(h1_ref, moe_ref, p_ref, gp_ref, wg_ref, wp_ref, gf_ref, o_ref, *, last_layer):
    h2 = h1_ref[...] + moe_ref[...]
    hn = _rms(h2, gp_ref[...]).astype(BF16)
    gate = jax.nn.sigmoid(jnp.dot(hn, wg_ref[...], preferred_element_type=F32))
    proj = jnp.dot(p_ref[...].astype(BF16), wp_ref[...], preferred_element_type=F32)
    h3 = h2 + gate * proj
    o_ref[...] = _rms(h3, gf_ref[...]) if last_layer else h3


def _ple(h1, moe, p2, gp, wg, wp, gf, last_layer, tm=1024):
    n, d = h1.shape
    dp = p2.shape[-1]
    row = pl.BlockSpec((tm, d), lambda i: (i, 0))
    vec = pl.BlockSpec((1, d), lambda i: (0, 0))
    return pl.pallas_call(
        functools.partial(_ple_kernel, last_layer=last_layer),
        grid=(n // tm,),
        in_specs=[row, row, pl.BlockSpec((tm, dp), lambda i: (i, 0)), vec,
                  pl.BlockSpec((d, d), lambda i: (0, 0)), pl.BlockSpec((dp, d), lambda i: (0, 0)), vec],
        out_specs=row,
        out_shape=jax.ShapeDtypeStruct((n, d), F32),
        compiler_params=_cparams(("parallel",)),
        name="ple",
    )(h1, moe, p2, gp, wg, wp, gf)


def kernel(x, p, g_mix, w_in, na_rpb, g_na_out, lam_q1, lam_k1, lam_q2, lam_k2, g_diff_out, w_out,
           g_moe, w_router, w_gate, w_up, w_down, g_ple, w_ple_gate, w_ple_proj, g_final):
    b, s, d = x.shape
    depth = w_in.shape[0]
    n = b * s
    na_w = NA_HEADS * HEAD_DIM
    cap = EC_CAPACITY_FACTOR * s // N_EXPERTS
    n_tile = s // MOE_TILE
    col_scale = np.ones((w_in.shape[-1],), np.float32)
    col_scale[:na_w] = HEAD_DIM ** -0.5 * LOG2E
    col_scale[3 * na_w:3 * na_w + DIFF_HEADS * 2 * HEAD_DIM] = HEAD_DIM ** -0.5 * LOG2E

    h = x.reshape(n, d)
    for i in range(depth):
        lam_init = 0.8 - 0.6 * math.exp(-0.3 * i)
        w_s = (w_in[i] * col_scale).astype(BF16)
        n_v = DIFF_HEADS * 2 * HEAD_DIM
        proj, vt = _inproj(h, g_mix[i][None], w_s[:, :-n_v], w_s[:, -n_v:].T, s)
        proj = proj.reshape(b, s, -1)
        o_na = _na_attention(proj, _na_bias_table(na_rpb[i]), g_na_out[i].reshape(1, na_w))
        lam_vecs = jnp.stack([lam_q1[i], lam_k1[i], lam_q2[i], lam_k2[i]]).astype(F32)
        o_df = _df_attention(proj, vt, lam_vecs, g_diff_out[i][None], lam_init)
        h1, xn, aff = _outproj(o_na.reshape(n, -1), o_df.reshape(n, -1), h, w_out[i].astype(BF16),
                               g_moe[i][None], w_router[i].astype(BF16))
        aff_t = aff.reshape(b, s, N_EXPERTS).transpose(0, 2, 1)
        slot, gate, pref = _route(aff_t, cap)
        last = jnp.full((b, N_EXPERTS, 1), cap, F32)
        tstart = jnp.concatenate([pref[:, :, ::MOE_TILE], last], axis=-1).astype(jnp.int32).reshape(-1)
        cstart = jnp.concatenate([pref[:, :, ::COMB_TILE], last], axis=-1).astype(jnp.int32).reshape(-1)
        slot_rows = slot.reshape(b, N_EXPERTS, n_tile, 1, MOE_TILE)
        gate_rows = gate.reshape(b, N_EXPERTS, n_tile, 1, MOE_TILE)
        y = _moe_ffn(tstart, xn.reshape(b, s, d), slot_rows, gate_rows, w_gate[i], w_up[i], w_down[i], cap)
        comb_rows = slot.reshape(b, N_EXPERTS, s // COMB_TILE, 1, COMB_TILE).transpose(0, 2, 1, 3, 4)
        pack_start = _window_start(pref[:, :, ::COMB_TILE].astype(jnp.int32), cap, COMB_PACK)
        rel = slot - jnp.repeat(pack_start, COMB_TILE, axis=-1).astype(F32)
        rel = jnp.where(slot >= 0, rel, 255.0).transpose(0, 2, 1).astype(BF16)
        moe = _moe_combine(cstart, y, comb_rows, rel, s)
        h = _ple(h1, moe.reshape(n, d), p[i].reshape(n, -1), g_ple[i][None],
                 w_ple_gate[i].astype(BF16), w_ple_proj[i].astype(BF16), g_final[None],
                 last_layer=(i == depth - 1))
    return h.reshape(b, s, d)
```

```python
import functools
import math

import jax
import jax.numpy as jnp
import numpy as np
from jax import lax
from jax.experimental import pallas as pl
from jax.experimental.pallas import tpu as pltpu

F32 = jnp.float32
BF16 = jnp.bfloat16

EPS = 1e-6
GRID_W = 64
HEAD_DIM = 64
NA_HEADS = 8
NA_KH_MAX = 8
NA_KW = 16
DIFF_HEADS = 4
N_EXPERTS = 16
EC_CAPACITY_FACTOR = 2
LANES = 128
MOE_TILE = 256
MOE_WIN = 128
DF_TK = 512
ROW_GROUPS = 4
COMB_TILE = 128
COMB_PACK = 64
NEG_BIG = -1e30
LOG2E = math.log2(math.e)
VMEM_LIMIT = 56 * 1024 * 1024


def _cparams(sem):
    return pltpu.CompilerParams(dimension_semantics=sem, vmem_limit_bytes=VMEM_LIMIT)


def _rms(x, g):
    return x * lax.rsqrt(jnp.mean(x * x, axis=-1, keepdims=True) + EPS) * g


def _inproj_kernel(x_ref, g_ref, w_ref, wvt_ref, o_ref, vt_ref, *, n_chunk):
    xn = _rms(x_ref[...], g_ref[...]).astype(BF16)
    cw = o_ref.shape[-1] // n_chunk
    for j in range(n_chunk):
        o_ref[:, j * cw:(j + 1) * cw] = jnp.dot(
            xn, w_ref[:, j * cw:(j + 1) * cw], preferred_element_type=F32).astype(BF16)
    vt_ref[0, 0] = lax.dot_general(wvt_ref[...], xn, (((1,), (1,)), ((), ())),
                                   preferred_element_type=F32).astype(BF16)


def _inproj(x2, g, w, wvt, seq, tm=DF_TK):
    n, d = x2.shape
    nout = w.shape[1]
    vw = wvt.shape[0]
    n_kb = seq // tm
    return pl.pallas_call(
        functools.partial(_inproj_kernel, n_chunk=nout // 512),
        grid=(n // tm,),
        in_specs=[pl.BlockSpec((tm, d), lambda i: (i, 0)),
                  pl.BlockSpec((1, d), lambda i: (0, 0)),
                  pl.BlockSpec((d, nout), lambda i: (0, 0)),
                  pl.BlockSpec((vw, d), lambda i: (0, 0))],
        out_specs=[pl.BlockSpec((tm, nout), lambda i: (i, 0)),
                   pl.BlockSpec((1, 1, vw, tm), lambda i: (i // n_kb, i % n_kb, 0, 0))],
        out_shape=[jax.ShapeDtypeStruct((n, nout), BF16),
                   jax.ShapeDtypeStruct((n // seq, n_kb, vw, tm), BF16)],
        compiler_params=_cparams(("parallel",)),
        name="inproj",
    )(x2, g, w, wvt)


def _na_kernel(q_ref, k_ref, v_ref, bias_ref, g_ref, o_ref,
               s0_scr, s1_scr, p0_scr, p1_scr, l0_scr, l1_scr, *, rows, kh, rb):
    lane = lax.broadcasted_iota(jnp.int32, (GRID_W, LANES), 1)
    lo = lane < HEAD_DIM
    nkeys = kh * GRID_W
    n_pair = NA_HEADS // 2
    zero = jnp.zeros((GRID_W, LANES), BF16)

    def geometry(r):
        rs = jnp.clip(r - kh // 2, 0, rows - kh)
        return (pl.multiple_of(r * GRID_W, GRID_W), pl.multiple_of(rs * GRID_W, GRID_W),
                rs - r + NA_KH_MAX - 1)

    def scores(r, s_scr):
        q0, k0, _ = geometry(r)
        for j in range(n_pair):
            cs = slice(j * LANES, (j + 1) * LANES)
            qj = q_ref[0, pl.ds(q0, GRID_W), cs]
            q2 = jnp.concatenate([jnp.where(lo, qj, zero), jnp.where(lo, zero, qj)], axis=0)
            s_scr[j] = lax.dot_general(q2, k_ref[0, pl.ds(k0, nkeys), cs], (((1,), (1,)), ((), ())),
                                       preferred_element_type=F32)

    def softmax(r, s_scr, p_scr, l_scr):
        _, _, off = geometry(r)
        for j in range(n_pair):
            for sb in range(2 * GRID_W // rb):
                rws = slice(sb * rb, (sb + 1) * rb)
                head = 2 * j + (sb * rb) // GRID_W
                brow = slice((sb * rb) % GRID_W, (sb * rb) % GRID_W + rb)
                chunks = [s_scr[j, rws, i * LANES:(i + 1) * LANES] + bias_ref[head, off + 2 * i, brow, :]
                          for i in range(kh // 2)]
                m = chunks[0]
                for c in chunks[1:]:
                    m = jnp.maximum(m, c)
                m = jnp.max(m, axis=-1, keepdims=True)
                lsum = None
                for i, c in enumerate(chunks):
                    pi = jnp.exp2(c - m)
                    p_scr[j, rws, i * LANES:(i + 1) * LANES] = pi.astype(BF16)
                    lsum = pi if lsum is None else lsum + pi
                l_scr[j, rws, :] = lsum

    def values(r, p_scr, l_scr):
        q0, k0, _ = geometry(r)
        for j in range(n_pair):
            cs = slice(j * LANES, (j + 1) * LANES)
            of = jnp.dot(p_scr[j], v_ref[0, pl.ds(k0, nkeys), cs], preferred_element_type=F32)
            of = of / jnp.sum(l_scr[j], axis=-1, keepdims=True)
            o = jnp.where(lo, of[:GRID_W], of[GRID_W:])
            sq = o * o
            ms_a = jnp.sum(jnp.where(lo, sq, 0.0), axis=-1, keepdims=True) * (1.0 / HEAD_DIM)
            ms_b = jnp.sum(jnp.where(lo, 0.0, sq), axis=-1, keepdims=True) * (1.0 / HEAD_DIM)
            inv = jnp.where(lo, lax.rsqrt(ms_a + EPS), lax.rsqrt(ms_b + EPS))
            o_ref[0, pl.ds(q0, GRID_W), cs] = (o * inv * g_ref[:, cs]).astype(BF16)

    p1_scr[...] = jnp.zeros_like(p1_scr)
    l1_scr[...] = jnp.ones_like(l1_scr)
    scores(0, s0_scr)

    def pair_body(i, carry):
        r = 2 * i
        scores(r + 1, s1_scr)
        softmax(r, s0_scr, p0_scr, l0_scr)
        values(jnp.maximum(r - 1, 0), p1_scr, l1_scr)
        scores(jnp.minimum(r + 2, rows - 1), s0_scr)
        softmax(r + 1, s1_scr, p1_scr, l1_scr)
        values(r, p0_scr, l0_scr)
        return carry

    lax.fori_loop(0, rows // 2, pair_body, 0)
    values(rows - 1, p1_scr, l1_scr)


def _na_attention(proj, bias, g, rb=32):
    b, s, _ = proj.shape
    rows = s // GRID_W
    kh = min(NA_KH_MAX, rows)
    assert rows % 2 == 0 and kh % 2 == 0 and GRID_W % rb == 0
    w = NA_HEADS * HEAD_DIM
    n_pair = NA_HEADS // 2
    stacked = 2 * GRID_W
    return pl.pallas_call(
        functools.partial(_na_kernel, rows=rows, kh=kh, rb=rb),
        grid=(b,),
        in_specs=[pl.BlockSpec((1, s, w), lambda i: (i, 0, 0)),
                  pl.BlockSpec((1, s, w), lambda i: (i, 0, 1)),
                  pl.BlockSpec((1, s, w), lambda i: (i, 0, 2)),
                  pl.BlockSpec(bias.shape, lambda i: (0, 0, 0, 0)),
                  pl.BlockSpec((1, w), lambda i: (0, 0))],
        out_specs=pl.BlockSpec((1, s, w), lambda i: (i, 0, 0)),
        out_shape=jax.ShapeDtypeStruct((b, s, w), BF16),
        scratch_shapes=[pltpu.VMEM((n_pair, stacked, kh * GRID_W), F32),
                        pltpu.VMEM((n_pair, stacked, kh * GRID_W), F32),
                        pltpu.VMEM((n_pair, stacked, kh * GRID_W), BF16),
                        pltpu.VMEM((n_pair, stacked, kh * GRID_W), BF16),
                        pltpu.VMEM((n_pair, stacked, LANES), F32),
                        pltpu.VMEM((n_pair, stacked, LANES), F32)],
        compiler_params=_cparams(("parallel",)),
        name="na_attn",
    )(proj, proj, proj, bias, g)


def _na_bias_table(rpb):
    qcol = np.arange(GRID_W)[:, None]
    kcol = np.arange(GRID_W)[None, :]
    cstart = np.clip(qcol - NA_KW // 2, 0, GRID_W - NA_KW)
    ok = (kcol >= cstart) & (kcol < cstart + NA_KW)
    dc = np.clip(kcol - qcol + NA_KW - 1, 0, 2 * NA_KW - 2)
    pick = (dc[None] == np.arange(2 * NA_KW - 1)[:, None, None]) & ok[None]
    t = jnp.einsum('hdc,cqk->hdqk', rpb.astype(F32) * LOG2E, jnp.asarray(pick, F32),
                   precision=lax.Precision.HIGHEST)
    t = t + jnp.asarray(np.where(ok, 0.0, NEG_BIG), F32)
    return jnp.concatenate([t[:, :-1], t[:, 1:]], axis=-1)


def _split3(x):
    hi = x.astype(BF16).astype(F32)
    rest = x - hi
    mid = rest.astype(BF16).astype(F32)
    return hi, mid, rest - mid


def _df_kernel(q_ref, k_ref, vt_ref, lam_ref, g_ref, o_ref,
               bias_scr, kaug_scr, q2_scr, s0_scr, s1_scr, p0_scr, p1_scr, a0_scr, a1_scr,
               m_scr, l_scr, acc_scr, *, tq, tk, kc, s_len, lam_init):
    h = pl.program_id(1)
    qi = pl.program_id(2)
    slope = jnp.exp2(-8.0 * (h + 1).astype(F32) / DIFF_HEADS) * LOG2E
    n_kb = s_len // tk
    n_par = tk // tq
    nt = (((1,), (1,)), ((), ()))

    @pl.when(qi == 0)
    def _():
        a = jnp.bitwise_and(lax.broadcasted_iota(jnp.int32, (tk, 2 * tq), 1), tq - 1)
        b = lax.broadcasted_iota(jnp.int32, (tk, 2 * tq), 0)
        dist = (a - b).astype(F32)
        for t in range(n_par):
            bias_scr[t] = slope * jnp.abs(dist + float(t * tq))
        klane = lax.broadcasted_iota(jnp.int32, (tk, LANES), 1)
        hi, mid, low = _split3(slope * lax.broadcasted_iota(jnp.int32, (tk, LANES), 0).astype(F32))
        kaug = jnp.where(klane < 3, 1.0, jnp.where(klane == 3, hi, jnp.where(
            klane == 4, mid, jnp.where(klane == 5, low, 0.0))))
        kaug_scr[...] = kaug.astype(BF16)

    lane = lax.broadcasted_iota(jnp.int32, (tq, LANES), 1)
    lo = lane < HEAD_DIM
    q = q_ref[0]
    zero = jnp.zeros_like(q)
    qm = (jnp.where(lo, q, zero), jnp.where(lo, zero, q))
    q2d = jnp.concatenate(qm, axis=0)
    hi, mid, low = _split3(-slope * lax.broadcasted_iota(jnp.int32, (tq, LANES), 0).astype(F32))
    qaug = jnp.where(lane == 0, hi, jnp.where(lane == 1, mid, jnp.where(
        lane == 2, low, jnp.where(lane < 6, 1.0, 0.0))))
    for v, sign in enumerate((1.0, -1.0)):
        aug = (sign * qaug).astype(BF16)
        q2_scr[v] = jnp.concatenate([jnp.concatenate([qm[0], aug], axis=1),
                                     jnp.concatenate([qm[1], aug], axis=1)], axis=0)

    k_diag = jnp.right_shift(qi, int(math.log2(n_par)))
    par = jnp.bitwise_and(qi, n_par - 1)

    def block(r):
        kb = k_diag + r
        wrapped = kb >= n_kb
        kb = jnp.where(wrapped, kb - n_kb, kb)
        return kb, pl.multiple_of(kb * tk, tk), wrapped

    def scores(r, s_scr):
        _, k0, wrapped = block(r)
        k_blk = k_ref[0, pl.ds(k0, tk), :]
        if r == 0:
            s_scr[...] = lax.dot_general(k_blk, q2d, nt, preferred_element_type=F32)
        else:
            s_scr[...] = lax.dot_general(jnp.concatenate([k_blk, kaug_scr[...]], axis=1),
                                         q2_scr[jnp.where(wrapped, 0, 1)], nt,
                                         preferred_element_type=F32)

    def fold8(x, op):
        out = x[0:8]
        for g in range(1, x.shape[0] // 8):
            out = op(out, x[8 * g:8 * (g + 1)])
        return out

    def softmax(r, s_scr, p_scr, a_scr):
        _, k0, wrapped = block(r)
        if r == 0:
            beta = 0.0
        else:
            cf = slope * (qi * tq - k0).astype(F32)
            beta = jnp.where(wrapped, -cf, cf)
        mx = None
        for c in range(tk // kc):
            rows = slice(c * kc, (c + 1) * kc)
            blk = s_scr[rows, :]
            if r == 0:
                blk = blk - bias_scr[par, rows, :]
                s_scr[rows, :] = blk
            part = fold8(blk, jnp.maximum)
            mx = part if mx is None else jnp.maximum(mx, part)
        m_old = m_scr[...]
        m_new = jnp.maximum(m_old, jnp.max(mx, axis=0, keepdims=True) + beta)
        alpha = jnp.exp2(m_old - m_new)
        shift = m_new - beta
        psum = None
        for c in range(tk // kc):
            rows = slice(c * kc, (c + 1) * kc)
            p = jnp.exp2(s_scr[rows, :] - shift)
            p_scr[rows, :] = p.astype(BF16)
            part = fold8(p, jnp.add)
            psum = part if psum is None else psum + part
        l_scr[...] = alpha * l_scr[...] + psum
        m_scr[...] = m_new
        a_scr[...] = alpha

    def values(r, p_scr, a_scr):
        kb, _, _ = block(r)
        acc_scr[...] = a_scr[...] * acc_scr[...] + jnp.dot(
            vt_ref[0, kb], p_scr[...], preferred_element_type=F32)

    m_scr[...] = jnp.full_like(m_scr, NEG_BIG)
    l_scr[...] = jnp.zeros_like(l_scr)
    acc_scr[...] = jnp.zeros_like(acc_scr)
    s_bufs, p_bufs, a_bufs = (s0_scr, s1_scr), (p0_scr, p1_scr), (a0_scr, a1_scr)
    scores(0, s0_scr)
    for r in range(n_kb):
        cur, oth = r % 2, 1 - r % 2
        if r + 1 < n_kb:
            scores(r + 1, s_bufs[oth])
        softmax(r, s_bufs[cur], p_bufs[cur], a_bufs[cur])
        if r >= 1:
            values(r - 1, p_bufs[oth], a_bufs[oth])
    values(n_kb - 1, p_bufs[(n_kb - 1) % 2], a_bufs[(n_kb - 1) % 2])
    o2t = acc_scr[...] / jnp.sum(l_scr[...], axis=0, keepdims=True)
    lv = lam_ref[...]
    lam = (jnp.exp(jnp.sum(lv[0:1] * lv[1:2], axis=-1, keepdims=True))
           - jnp.exp(jnp.sum(lv[2:3] * lv[3:4], axis=-1, keepdims=True)) + lam_init)
    o = (o2t[:, :tq] - lam * o2t[:, tq:]).T
    o_ref[0] = (_rms(o, g_ref[...]) * (1.0 - lam_init)).astype(BF16)


def _df_attention(proj, vt, lam_vecs, g, lam_init, tq=512, tk=DF_TK, kc=128):
    b, s, _ = proj.shape
    tq = min(tq, s)
    assert tk % tq == 0 and s % tk == 0 and tk % kc == 0
    n_kb = s // tk
    qb, kb = 12, 16
    return pl.pallas_call(
        functools.partial(_df_kernel, tq=tq, tk=tk, kc=kc, s_len=s, lam_init=lam_init),
        grid=(b, DIFF_HEADS, s // tq),
        in_specs=[pl.BlockSpec((1, tq, LANES), lambda i, h, j: (i, j, qb + h)),
                  pl.BlockSpec((1, s, LANES), lambda i, h, j: (i, 0, kb + h)),
                  pl.BlockSpec((1, n_kb, LANES, tk), lambda i, h, j: (i, 0, h, 0)),
                  pl.BlockSpec((4, HEAD_DIM), lambda i, h, j: (0, 0)),
                  pl.BlockSpec((1, LANES), lambda i, h, j: (0, 0))],
        out_specs=pl.BlockSpec((1, tq, LANES), lambda i, h, j: (i, j, h)),
        out_shape=jax.ShapeDtypeStruct((b, s, DIFF_HEADS * LANES), BF16),
        scratch_shapes=[pltpu.VMEM((tk // tq, tk, 2 * tq), F32),
                        pltpu.VMEM((tk, LANES), BF16),
                        pltpu.VMEM((2, 2 * tq, 2 * LANES), BF16),
                        pltpu.VMEM((tk, 2 * tq), F32),
                        pltpu.VMEM((tk, 2 * tq), F32),
                        pltpu.VMEM((tk, 2 * tq), BF16),
                        pltpu.VMEM((tk, 2 * tq), BF16),
                        pltpu.VMEM((1, 2 * tq), F32),
                        pltpu.VMEM((1, 2 * tq), F32),
                        pltpu.VMEM((1, 2 * tq), F32),
                        pltpu.VMEM((8, 2 * tq), F32),
                        pltpu.VMEM((LANES, 2 * tq), F32)],
        compiler_params=_cparams(("parallel", "parallel", "arbitrary")),
        name="df_attn",
    )(proj, proj, vt, lam_vecs, g)


def _outproj_kernel(ona_ref, odf_ref, x_ref, wo_ref, g_ref, wr_ref, h1_ref, xn_ref, aff_ref):
    half = ona_ref.shape[-1]
    rg = h1_ref.shape[0] // ROW_GROUPS
    for c in range(ROW_GROUPS):
        rows = slice(c * rg, (c + 1) * rg)
        acc = jnp.dot(ona_ref[rows, :], wo_ref[:half, :], preferred_element_type=F32)
        acc = acc + jnp.dot(odf_ref[rows, :], wo_ref[half:, :], preferred_element_type=F32)
        h1 = x_ref[rows, :] + acc
        h1_ref[rows, :] = h1
        xn = _rms(h1, g_ref[...]).astype(BF16)
        xn_ref[rows, :] = xn
        logits = jnp.dot(xn, wr_ref[...], preferred_element_type=F32)
        z = jnp.exp(logits - jnp.max(logits, axis=-1, keepdims=True))
        aff_ref[rows, :] = z / jnp.sum(z, axis=-1, keepdims=True)


def _outproj(o_na, o_df, x2, wo, g, wr, tm=1024):
    n, d = x2.shape
    half = o_na.shape[-1]
    e = wr.shape[-1]
    return pl.pallas_call(
        _outproj_kernel,
        grid=(n // tm,),
        in_specs=[pl.BlockSpec((tm, half), lambda i: (i, 0)),
                  pl.BlockSpec((tm, half), lambda i: (i, 0)),
                  pl.BlockSpec((tm, d), lambda i: (i, 0)),
                  pl.BlockSpec((2 * half, d), lambda i: (0, 0)),
                  pl.BlockSpec((1, d), lambda i: (0, 0)),
                  pl.BlockSpec((d, e), lambda i: (0, 0))],
        out_specs=[pl.BlockSpec((tm, d), lambda i: (i, 0)),
                   pl.BlockSpec((tm, d), lambda i: (i, 0)),
                   pl.BlockSpec((tm, e), lambda i: (i, 0))],
        out_shape=[jax.ShapeDtypeStruct((n, d), F32),
                   jax.ShapeDtypeStruct((n, d), BF16),
                   jax.ShapeDtypeStruct((n, e), F32)],
        compiler_params=_cparams(("parallel",)),
        name="outproj",
    )(o_na, o_df, x2, wo, g, wr)


def _excl_prefix(mask, upper):
    e, s = mask.shape
    mb = jnp.where(mask, 1.0, 0.0).astype(BF16)
    carry = jnp.zeros((e, 1), F32)
    outs = []
    for c in range(s // LANES):
        xc = mb[:, c * LANES:(c + 1) * LANES]
        outs.append(jnp.dot(xc, upper, preferred_element_type=F32) + carry)
        carry = carry + jnp.sum(xc.astype(F32), axis=-1, keepdims=True)
    return jnp.concatenate(outs, axis=1)


def _route_kernel(aff_ref, slot_ref, gate_ref, pref_ref, *, cap):
    a = aff_ref[0]
    e = a.shape[0]

    def count_ge(t):
        return jnp.sum(jnp.where(a >= t, 1, 0), axis=-1, keepdims=True)

    def bit_body(i, bits):
        trial = bits | jnp.left_shift(jnp.int32(1), 30 - i)
        return jnp.where(count_ge(pltpu.bitcast(trial, F32)) >= cap, trial, bits)

    thr = pltpu.bitcast(lax.fori_loop(0, 31, bit_body, jnp.zeros((e, 1), jnp.int32)), F32)
    step0 = thr
    for j in range(1, 21):
        trial = thr + step0 * (2.0 ** -(24 + j))
        thr = jnp.where(count_ge(trial) >= cap, trial, thr)
    gt = a > thr
    eq = a == thr
    need = (cap - jnp.sum(jnp.where(gt, 1, 0), axis=-1, keepdims=True)).astype(F32)
    r = lax.broadcasted_iota(jnp.int32, (LANES, LANES), 0)
    c = lax.broadcasted_iota(jnp.int32, (LANES, LANES), 1)
    upper = jnp.where(r < c, 1.0, 0.0).astype(BF16)
    sel = gt | (eq & (_excl_prefix(eq, upper) < need))
    pref = _excl_prefix(sel, upper)
    pref_ref[0] = pref
    slot_ref[0] = jnp.where(sel, pref, -1.0)
    gate_ref[0] = jnp.where(sel, a, 0.0)


def _route(aff_t, cap):
    b, e, s = aff_t.shape
    spec = pl.BlockSpec((1, e, s), lambda i: (i, 0, 0))
    shp = jax.ShapeDtypeStruct((b, e, s), F32)
    return pl.pallas_call(
        functools.partial(_route_kernel, cap=cap),
        grid=(b,),
        in_specs=[spec],
        out_specs=[spec, spec, spec],
        out_shape=[shp, shp, shp],
        compiler_params=_cparams(("parallel",)),
        name="route",
    )(aff_t)


def _window_range(ts_ref, base, t):
    s0 = ts_ref[base + t]
    s1 = ts_ref[base + t + 1]
    w_lo = jnp.right_shift(s0, int(math.log2(MOE_WIN)))
    w_hi = jnp.where(s1 > s0, jnp.right_shift(s1 - 1, int(math.log2(MOE_WIN))) + 1, w_lo)
    return w_lo, w_hi


def _moe_ffn_kernel(ts_ref, xn_ref, slot_ref, gate_ref, wg_ref, wu_ref, wd_ref, y_ref, xs_acc, gs_acc,
                    *, n_tile, n_fchunk):
    b = pl.program_id(0)
    e = pl.program_id(1)
    base = (b * pl.num_programs(1) + e) * (n_tile + 1)
    xs_acc[...] = jnp.zeros_like(xs_acc)
    gs_acc[...] = jnp.zeros_like(gs_acc)
    sub = lax.broadcasted_iota(jnp.int32, (MOE_WIN, MOE_TILE), 0).astype(F32)
    cap = xs_acc.shape[0]
    last_win = cap // MOE_WIN - 1

    def add_rows(r0, first_slot, srow, grow, xt):
        match = srow - first_slot.astype(F32) == sub
        onehot = jnp.where(match, 1.0, 0.0).astype(BF16)
        xs_acc[pl.ds(r0, MOE_WIN), :] += jnp.dot(onehot, xt, preferred_element_type=F32)
        gs_acc[pl.ds(r0, MOE_WIN), :] += jnp.sum(jnp.where(match, grow, 0.0), axis=-1, keepdims=True)

    def compact(w, srow, grow, xt):
        add_rows(pl.multiple_of(jnp.minimum(w, last_win) * MOE_WIN, MOE_WIN), w * MOE_WIN, srow, grow, xt)

    n_max = ts_ref[base + 1] - ts_ref[base]
    for t in range(1, n_tile):
        n_max = jnp.maximum(n_max, ts_ref[base + t + 1] - ts_ref[base + t])
    single = n_max <= MOE_WIN - 7

    @pl.when(single)
    def _():
        for t in range(n_tile):
            s0 = ts_ref[base + t]
            start = jnp.minimum(jnp.left_shift(jnp.right_shift(s0, 3), 3), cap - MOE_WIN)
            add_rows(pl.multiple_of(start, 8), start, slot_ref[0, 0, t], gate_ref[0, 0, t],
                     xn_ref[0, t * MOE_TILE:(t + 1) * MOE_TILE, :])

    @pl.when(jnp.logical_not(single))
    def _():
        def tile_body(t, carry):
            w_lo, w_hi = _window_range(ts_ref, base, t)
            srow = slot_ref[0, 0, t]
            grow = gate_ref[0, 0, t]
            xt = xn_ref[0, pl.ds(pl.multiple_of(t * MOE_TILE, MOE_TILE), MOE_TILE), :]

            def win_body(w, c2):
                compact(w, srow, grow, xt)
                return c2

            lax.fori_loop(w_lo + 2, w_hi, win_body, 0)
            return carry

        lax.fori_loop(0, n_tile, tile_body, 0)

        for t in range(n_tile):
            w_lo = jnp.right_shift(ts_ref[base + t], int(math.log2(MOE_WIN)))
            srow = slot_ref[0, 0, t]
            grow = gate_ref[0, 0, t]
            xt = xn_ref[0, t * MOE_TILE:(t + 1) * MOE_TILE, :]
            compact(w_lo, srow, grow, xt)
            compact(w_lo + 1, srow, grow, xt)

    xs = xs_acc[...].astype(BF16)
    f = wg_ref.shape[-1]
    fc = f // n_fchunk
    y = None
    for j in range(n_fchunk):
        fs = slice(j * fc, (j + 1) * fc)
        gt = jnp.dot(xs, wg_ref[0, :, fs].astype(BF16), preferred_element_type=F32)
        up = jnp.dot(xs, wu_ref[0, :, fs].astype(BF16), preferred_element_type=F32)
        hid = (gt * jax.nn.sigmoid(gt) * up).astype(BF16)
        part = jnp.dot(hid, wd_ref[0, fs, :].astype(BF16), preferred_element_type=F32)
        y = part if y is None else y + part
    y_ref[0, 0] = (y * gs_acc[...]).astype(BF16)


def _moe_ffn(tstart, xn, slot_rows, gate_rows, wg, wu, wd, cap):
    b, s, d = xn.shape
    e, _, f = wg.shape
    n_tile = s // MOE_TILE
    row_spec = pl.BlockSpec((1, 1, n_tile, 1, MOE_TILE), lambda i, j, ts: (i, j, 0, 0, 0))
    grid_spec = pltpu.PrefetchScalarGridSpec(
        num_scalar_prefetch=1,
        grid=(b, e),
        in_specs=[pl.BlockSpec((1, s, d), lambda i, j, ts: (i, 0, 0)),
                  row_spec, row_spec,
                  pl.BlockSpec((1, d, f), lambda i, j, ts: (j, 0, 0)),
                  pl.BlockSpec((1, d, f), lambda i, j, ts: (j, 0, 0)),
                  pl.BlockSpec((1, f, d), lambda i, j, ts: (j, 0, 0))],
        out_specs=pl.BlockSpec((1, 1, cap, d), lambda i, j, ts: (i, j, 0, 0)),
        scratch_shapes=[pltpu.VMEM((cap, d), F32), pltpu.VMEM((cap, 1), F32)],
    )
    return pl.pallas_call(
        functools.partial(_moe_ffn_kernel, n_tile=n_tile, n_fchunk=max(1, f // 256)),
        grid_spec=grid_spec,
        out_shape=jax.ShapeDtypeStruct((b, e, cap, d), BF16),
        compiler_params=_cparams(("parallel", "arbitrary")),
        name="moe_ffn",
    )(tstart, xn, slot_rows, gate_rows, wg, wu, wd)


def _window_start(s0, cap, win):
    return jnp.minimum(jnp.left_shift(jnp.right_shift(s0, 4), 4), cap - win)


def _moe_comb_kernel(ts_ref, y_ref, slot_ref, rel_ref, o_ref, ycat_scr, *, n_tile, win):
    b = pl.program_id(0)
    t = pl.program_id(1)
    n_exp, cap = y_ref.shape[1], y_ref.shape[2]

    def first_slot(e):
        return ts_ref[(b * n_exp + e) * (n_tile + 1) + t]

    n_max = ts_ref[b * n_exp * (n_tile + 1) + t + 1] - first_slot(0)
    for e in range(1, n_exp):
        n_max = jnp.maximum(n_max, ts_ref[(b * n_exp + e) * (n_tile + 1) + t + 1] - first_slot(e))
    packed = n_max <= COMB_PACK - 15

    @pl.when(packed)
    def _():
        for e in range(n_exp):
            start = pl.multiple_of(_window_start(first_slot(e), cap, COMB_PACK), 16)
            ycat_scr[e * COMB_PACK:(e + 1) * COMB_PACK, :] = y_ref[0, e, pl.ds(start, COMB_PACK), :]
        width = n_exp * COMB_PACK
        shift = int(math.log2(COMB_PACK))
        lane_e = jnp.right_shift(lax.broadcasted_iota(jnp.int32, (n_exp, width), 1), shift)
        spread = jnp.where(lane_e == lax.broadcasted_iota(jnp.int32, (n_exp, width), 0), 1.0, 0.0)
        rel = jnp.dot(rel_ref[0], spread.astype(BF16), preferred_element_type=F32)
        lane_j = jnp.bitwise_and(lax.broadcasted_iota(jnp.int32, (COMB_TILE, width), 1), COMB_PACK - 1)
        onehot = jnp.where(rel == lane_j.astype(F32), 1.0, 0.0).astype(BF16)
        o_ref[0] = jnp.dot(onehot, ycat_scr[...], preferred_element_type=F32)

    @pl.when(jnp.logical_not(packed))
    def _():
        lane = lax.broadcasted_iota(jnp.int32, (COMB_TILE, win), 1).astype(F32)
        acc = None
        for e in range(n_exp):
            start = pl.multiple_of(_window_start(first_slot(e), cap, win), 16)
            scol = jnp.broadcast_to(slot_ref[0, 0, e], (LANES, COMB_TILE)).T - start.astype(F32)
            scol = jnp.tile(scol, (1, win // LANES))
            onehot = jnp.where(scol == lane, 1.0, 0.0).astype(BF16)
            part = jnp.dot(onehot, y_ref[0, e, pl.ds(start, win), :], preferred_element_type=F32)
            acc = part if acc is None else acc + part
        o_ref[0] = acc


def _moe_combine(tstart, y, slot_rows, rel, s):
    b, e, cap, d = y.shape
    n_tile = s // COMB_TILE
    win = min(cap, 2 * COMB_TILE)
    assert win == cap or win >= COMB_TILE + 15
    grid_spec = pltpu.PrefetchScalarGridSpec(
        num_scalar_prefetch=1,
        grid=(b, n_tile),
        in_specs=[pl.BlockSpec((1, e, cap, d), lambda i, j, ts: (i, 0, 0, 0)),
                  pl.BlockSpec((1, 1, e, 1, COMB_TILE), lambda i, j, ts: (i, j, 0, 0, 0)),
                  pl.BlockSpec((1, COMB_TILE, e), lambda i, j, ts: (i, j, 0))],
        out_specs=pl.BlockSpec((1, COMB_TILE, d), lambda i, j, ts: (i, j, 0)),
        scratch_shapes=[pltpu.VMEM((e * COMB_PACK, d), BF16)],
    )
    return pl.pallas_call(
        functools.partial(_moe_comb_kernel, n_tile=n_tile, win=win),
        grid_spec=grid_spec,
        out_shape=jax.ShapeDtypeStruct((b, s, d), F32),
        compiler_params=_cparams(("parallel", "arbitrary")),
        name="moe_comb",
    )(tstart, y, slot_rows, rel)


def _ple_kernel(h1_ref, moe_ref, p_ref, gp_ref, wg_ref, wp_ref, gf_ref, o_ref, *, last_layer):
    rg = o_ref.shape[0] // ROW_GROUPS
    for c in range(ROW_GROUPS):
        rows = slice(c * rg, (c + 1) * rg)
        h2 = h1_ref[rows, :] + moe_ref[rows, :]
        hn = _rms(h2, gp_ref[...]).astype(BF16)
        gate = jax.nn.sigmoid(jnp.dot(hn, wg_ref[...], preferred_element_type=F32))
        proj = jnp.dot(p_ref[rows, :].astype(BF16), wp_ref[...], preferred_element_type=F32)
        h3 = h2 + gate * proj
        o_ref[rows, :] = _rms(h3, gf_ref[...]) if last_layer else h3


def _ple(h1, moe, p2, gp, wg, wp, gf, last_layer, tm=1024):
    n, d = h1.shape
    dp = p2.shape[-1]
    row = pl.BlockSpec((tm, d), lambda i: (i, 0))
    vec = pl.BlockSpec((1, d), lambda i: (0, 0))
    return pl.pallas_call(
        functools.partial(_ple_kernel, last_layer=last_layer),
        grid=(n // tm,),
        in_specs=[row, row, pl.BlockSpec((tm, dp), lambda i: (i, 0)), vec,
                  pl.BlockSpec((d, d), lambda i: (0, 0)), pl.BlockSpec((dp, d), lambda i: (0, 0)), vec],
        out_specs=row,
        out_shape=jax.ShapeDtypeStruct((n, d), F32),
        compiler_params=_cparams(("parallel",)),
        name="ple",
    )(h1, moe, p2, gp, wg, wp, gf)


def kernel(x, p, g_mix, w_in, na_rpb, g_na_out, lam_q1, lam_k1, lam_q2, lam_k2, g_diff_out, w_out,
           g_moe, w_router, w_gate, w_up, w_down, g_ple, w_ple_gate, w_ple_proj, g_final):
    b, s, d = x.shape
    depth = w_in.shape[0]
    n = b * s
    na_w = NA_HEADS * HEAD_DIM
    cap = EC_CAPACITY_FACTOR * s // N_EXPERTS
    n_tile = s // MOE_TILE
    col_scale = np.ones((w_in.shape[-1],), np.float32)
    col_scale[:na_w] = HEAD_DIM ** -0.5 * LOG2E
    col_scale[3 * na_w:3 * na_w + DIFF_HEADS * 2 * HEAD_DIM] = HEAD_DIM ** -0.5 * LOG2E

    h = x.reshape(n, d)
    for i in range(depth):
        lam_init = 0.8 - 0.6 * math.exp(-0.3 * i)
        w_s = (w_in[i] * col_scale).astype(BF16)
        n_v = DIFF_HEADS * 2 * HEAD_DIM
        proj, vt = _inproj(h, g_mix[i][None], w_s[:, :-n_v], w_s[:, -n_v:].T, s)
        proj = proj.reshape(b, s, -1)
        o_na = _na_attention(proj, _na_bias_table(na_rpb[i]), g_na_out[i].reshape(1, na_w))
        lam_vecs = jnp.stack([lam_q1[i], lam_k1[i], lam_q2[i], lam_k2[i]]).astype(F32)
        o_df = _df_attention(proj, vt, lam_vecs, g_diff_out[i][None], lam_init)
        h1, xn, aff = _outproj(o_na.reshape(n, -1), o_df.reshape(n, -1), h, w_out[i].astype(BF16),
                               g_moe[i][None], w_router[i].astype(BF16))
        aff_t = aff.reshape(b, s, N_EXPERTS).transpose(0, 2, 1)
        slot, gate, pref = _route(aff_t, cap)
        last = jnp.full((b, N_EXPERTS, 1), cap, F32)
        tstart = jnp.concatenate([pref[:, :, ::MOE_TILE], last], axis=-1).astype(jnp.int32).reshape(-1)
        cstart = jnp.concatenate([pref[:, :, ::COMB_TILE], last], axis=-1).astype(jnp.int32).reshape(-1)
        slot_rows = slot.reshape(b, N_EXPERTS, n_tile, 1, MOE_TILE)
        gate_rows = gate.reshape(b, N_EXPERTS, n_tile, 1, MOE_TILE)
        y = _moe_ffn(tstart, xn.reshape(b, s, d), slot_rows, gate_rows, w_gate[i], w_up[i], w_down[i], cap)
        comb_rows = slot.reshape(b, N_EXPERTS, s // COMB_TILE, 1, COMB_TILE).transpose(0, 2, 1, 3, 4)
        pack_start = _window_start(pref[:, :, ::COMB_TILE].astype(jnp.int32), cap, COMB_PACK)
        rel = slot - jnp.repeat(pack_start, COMB_TILE, axis=-1).astype(F32)
        rel = jnp.where(slot >= 0, rel, 255.0).transpose(0, 2, 1).astype(BF16)
        moe = _moe_combine(cstart, y, comb_rows, rel, s)
        h = _ple(h1, moe.reshape(n, d), p[i].reshape(n, -1), g_ple[i][None],
                 w_ple_gate[i].astype(BF16), w_ple_proj[i].astype(BF16), g_final[None],
                 last_layer=(i == depth - 1))
    return h.reshape(b, s, d)
```

```python
import functools
import math

import jax
import jax.numpy as jnp
import numpy as np
from jax import lax
from jax.experimental import pallas as pl
from jax.experimental.pallas import tpu as pltpu

F32 = jnp.float32
BF16 = jnp.bfloat16

EPS = 1e-6
GRID_W = 64
HEAD_DIM = 64
NA_HEADS = 8
NA_KH_MAX = 8
NA_KW = 16
DIFF_HEADS = 4
N_EXPERTS = 16
EC_CAPACITY_FACTOR = 2
LANES = 128
MOE_TILE = 256
MOE_WIN = 128
DF_TK = 1024
COMB_TILE = 128
COMB_PACK = 64
NEG_BIG = -1e30
LOG2E = math.log2(math.e)
VMEM_LIMIT = 56 * 1024 * 1024


def _cparams(sem):
    return pltpu.CompilerParams(dimension_semantics=sem, vmem_limit_bytes=VMEM_LIMIT)


def _rms(x, g):
    return x * lax.rsqrt(jnp.mean(x * x, axis=-1, keepdims=True) + EPS) * g


def _inproj_kernel(x_ref, g_ref, w_ref, wvt_ref, o_ref, vt_ref, *, n_chunk):
    xn = _rms(x_ref[...], g_ref[...]).astype(BF16)
    cw = o_ref.shape[-1] // n_chunk
    for j in range(n_chunk):
        o_ref[:, j * cw:(j + 1) * cw] = jnp.dot(
            xn, w_ref[:, j * cw:(j + 1) * cw], preferred_element_type=F32).astype(BF16)
    vt_ref[0, 0] = lax.dot_general(wvt_ref[...], xn, (((1,), (1,)), ((), ())),
                                   preferred_element_type=F32).astype(BF16)


def _inproj(x2, g, w, wvt, seq, tm=DF_TK):
    n, d = x2.shape
    nout = w.shape[1]
    vw = wvt.shape[0]
    n_kb = seq // tm
    return pl.pallas_call(
        functools.partial(_inproj_kernel, n_chunk=nout // 512),
        grid=(n // tm,),
        in_specs=[pl.BlockSpec((tm, d), lambda i: (i, 0)),
                  pl.BlockSpec((1, d), lambda i: (0, 0)),
                  pl.BlockSpec((d, nout), lambda i: (0, 0)),
                  pl.BlockSpec((vw, d), lambda i: (0, 0))],
        out_specs=[pl.BlockSpec((tm, nout), lambda i: (i, 0)),
                   pl.BlockSpec((1, 1, vw, tm), lambda i: (i // n_kb, i % n_kb, 0, 0))],
        out_shape=[jax.ShapeDtypeStruct((n, nout), BF16),
                   jax.ShapeDtypeStruct((n // seq, n_kb, vw, tm), BF16)],
        compiler_params=_cparams(("parallel",)),
        name="inproj",
    )(x2, g, w, wvt)


def _na_kernel(q_ref, k_ref, v_ref, bias_ref, g_ref, o_ref,
               s0_scr, s1_scr, p0_scr, p1_scr, l0_scr, l1_scr, *, rows, kh, rb):
    lane = lax.broadcasted_iota(jnp.int32, (GRID_W, LANES), 1)
    lo = lane < HEAD_DIM
    nkeys = kh * GRID_W
    n_pair = NA_HEADS // 2
    zero = jnp.zeros((GRID_W, LANES), BF16)

    def geometry(r):
        rs = jnp.clip(r - kh // 2, 0, rows - kh)
        return (pl.multiple_of(r * GRID_W, GRID_W), pl.multiple_of(rs * GRID_W, GRID_W),
                rs - r + NA_KH_MAX - 1)

    def scores(r, s_scr):
        q0, k0, _ = geometry(r)
        for j in range(n_pair):
            cs = slice(j * LANES, (j + 1) * LANES)
            qj = q_ref[0, pl.ds(q0, GRID_W), cs]
            q2 = jnp.concatenate([jnp.where(lo, qj, zero), jnp.where(lo, zero, qj)], axis=0)
            s_scr[j] = lax.dot_general(q2, k_ref[0, pl.ds(k0, nkeys), cs], (((1,), (1,)), ((), ())),
                                       preferred_element_type=F32)

    def softmax(r, s_scr, p_scr, l_scr):
        _, _, off = geometry(r)
        for j in range(n_pair):
            for sb in range(2 * GRID_W // rb):
                rws = slice(sb * rb, (sb + 1) * rb)
                head = 2 * j + (sb * rb) // GRID_W
                brow = slice((sb * rb) % GRID_W, (sb * rb) % GRID_W + rb)
                chunks = [s_scr[j, rws, i * LANES:(i + 1) * LANES] + bias_ref[head, off + 2 * i, brow, :]
                          for i in range(kh // 2)]
                m = chunks[0]
                for c in chunks[1:]:
                    m = jnp.maximum(m, c)
                m = jnp.max(m, axis=-1, keepdims=True)
                lsum = None
                for i, c in enumerate(chunks):
                    pi = jnp.exp2(c - m)
                    p_scr[j, rws, i * LANES:(i + 1) * LANES] = pi.astype(BF16)
                    lsum = pi if lsum is None else lsum + pi
                l_scr[j, rws, :] = lsum

    def values(r, p_scr, l_scr):
        q0, k0, _ = geometry(r)
        for j in range(n_pair):
            cs = slice(j * LANES, (j + 1) * LANES)
            of = jnp.dot(p_scr[j], v_ref[0, pl.ds(k0, nkeys), cs], preferred_element_type=F32)
            of = of / jnp.sum(l_scr[j], axis=-1, keepdims=True)
            o = jnp.where(lo, of[:GRID_W], of[GRID_W:])
            sq = o * o
            ms_a = jnp.sum(jnp.where(lo, sq, 0.0), axis=-1, keepdims=True) * (1.0 / HEAD_DIM)
            ms_b = jnp.sum(jnp.where(lo, 0.0, sq), axis=-1, keepdims=True) * (1.0 / HEAD_DIM)
            inv = jnp.where(lo, lax.rsqrt(ms_a + EPS), lax.rsqrt(ms_b + EPS))
            o_ref[0, pl.ds(q0, GRID_W), cs] = (o * inv * g_ref[:, cs]).astype(BF16)

    p1_scr[...] = jnp.zeros_like(p1_scr)
    l1_scr[...] = jnp.ones_like(l1_scr)
    scores(0, s0_scr)

    def pair_body(i, carry):
        r = 2 * i
        scores(r + 1, s1_scr)
        softmax(r, s0_scr, p0_scr, l0_scr)
        values(jnp.maximum(r - 1, 0), p1_scr, l1_scr)
        scores(jnp.minimum(r + 2, rows - 1), s0_scr)
        softmax(r + 1, s1_scr, p1_scr, l1_scr)
        values(r, p0_scr, l0_scr)
        return carry

    lax.fori_loop(0, rows // 2, pair_body, 0)
    values(rows - 1, p1_scr, l1_scr)


def _na_attention(proj, bias, g, rb=32):
    b, s, _ = proj.shape
    rows = s // GRID_W
    kh = min(NA_KH_MAX, rows)
    assert rows % 2 == 0 and kh % 2 == 0 and GRID_W % rb == 0
    w = NA_HEADS * HEAD_DIM
    n_pair = NA_HEADS // 2
    stacked = 2 * GRID_W
    return pl.pallas_call(
        functools.partial(_na_kernel, rows=rows, kh=kh, rb=rb),
        grid=(b,),
        in_specs=[pl.BlockSpec((1, s, w), lambda i: (i, 0, 0)),
                  pl.BlockSpec((1, s, w), lambda i: (i, 0, 1)),
                  pl.BlockSpec((1, s, w), lambda i: (i, 0, 2)),
                  pl.BlockSpec(bias.shape, lambda i: (0, 0, 0, 0)),
                  pl.BlockSpec((1, w), lambda i: (0, 0))],
        out_specs=pl.BlockSpec((1, s, w), lambda i: (i, 0, 0)),
        out_shape=jax.ShapeDtypeStruct((b, s, w), BF16),
        scratch_shapes=[pltpu.VMEM((n_pair, stacked, kh * GRID_W), F32),
                        pltpu.VMEM((n_pair, stacked, kh * GRID_W), F32),
                        pltpu.VMEM((n_pair, stacked, kh * GRID_W), BF16),
                        pltpu.VMEM((n_pair, stacked, kh * GRID_W), BF16),
                        pltpu.VMEM((n_pair, stacked, LANES), F32),
                        pltpu.VMEM((n_pair, stacked, LANES), F32)],
        compiler_params=_cparams(("parallel",)),
        name="na_attn",
    )(proj, proj, proj, bias, g)


def _na_bias_table(rpb):
    qcol = np.arange(GRID_W)[:, None]
    kcol = np.arange(GRID_W)[None, :]
    cstart = np.clip(qcol - NA_KW // 2, 0, GRID_W - NA_KW)
    ok = (kcol >= cstart) & (kcol < cstart + NA_KW)
    dc = np.clip(kcol - qcol + NA_KW - 1, 0, 2 * NA_KW - 2)
    pick = (dc[None] == np.arange(2 * NA_KW - 1)[:, None, None]) & ok[None]
    t = jnp.einsum('hdc,cqk->hdqk', rpb.astype(F32) * LOG2E, jnp.asarray(pick, F32),
                   precision=lax.Precision.HIGHEST)
    t = t + jnp.asarray(np.where(ok, 0.0, NEG_BIG), F32)
    return jnp.concatenate([t[:, :-1], t[:, 1:]], axis=-1)


def _split3(x):
    hi = x.astype(BF16).astype(F32)
    rest = x - hi
    mid = rest.astype(BF16).astype(F32)
    return hi, mid, rest - mid


def _df_kernel(q_ref, k_ref, vt_ref, lam_ref, g_ref, o_ref,
               bias_scr, kaug_scr, q2_scr, s0_scr, s1_scr, p0_scr, p1_scr, a0_scr, a1_scr,
               m_scr, l_scr, acc_scr, *, tq, tk, kc, s_len, lam_init):
    h = pl.program_id(1)
    qi = pl.program_id(2)
    slope = jnp.exp2(-8.0 * (h + 1).astype(F32) / DIFF_HEADS) * LOG2E
    n_kb = s_len // tk
    n_par = tk // tq
    nt = (((1,), (1,)), ((), ()))

    @pl.when(qi == 0)
    def _():
        a = jnp.bitwise_and(lax.broadcasted_iota(jnp.int32, (tk, 2 * tq), 1), tq - 1)
        b = lax.broadcasted_iota(jnp.int32, (tk, 2 * tq), 0)
        dist = (a - b).astype(F32)
        for t in range(n_par):
            bias_scr[t] = slope * jnp.abs(dist + float(t * tq))
        klane = lax.broadcasted_iota(jnp.int32, (tk, LANES), 1)
        hi, mid, low = _split3(slope * lax.broadcasted_iota(jnp.int32, (tk, LANES), 0).astype(F32))
        kaug = jnp.where(klane < 3, 1.0, jnp.where(klane == 3, hi, jnp.where(
            klane == 4, mid, jnp.where(klane == 5, low, 0.0))))
        kaug_scr[...] = kaug.astype(BF16)

    lane = lax.broadcasted_iota(jnp.int32, (tq, LANES), 1)
    lo = lane < HEAD_DIM
    q = q_ref[0]
    zero = jnp.zeros_like(q)
    qm = (jnp.where(lo, q, zero), jnp.where(lo, zero, q))
    q2d = jnp.concatenate(qm, axis=0)
    hi, mid, low = _split3(-slope * lax.broadcasted_iota(jnp.int32, (tq, LANES), 0).astype(F32))
    qaug = jnp.where(lane == 0, hi, jnp.where(lane == 1, mid, jnp.where(
        lane == 2, low, jnp.where(lane < 6, 1.0, 0.0))))
    for v, sign in enumerate((1.0, -1.0)):
        aug = (sign * qaug).astype(BF16)
        q2_scr[v] = jnp.concatenate([jnp.concatenate([qm[0], aug], axis=1),
                                     jnp.concatenate([qm[1], aug], axis=1)], axis=0)

    k_diag = jnp.right_shift(qi, int(math.log2(n_par)))
    par = jnp.bitwise_and(qi, n_par - 1)

    def block(r):
        kb = k_diag + r
        wrapped = kb >= n_kb
        kb = jnp.where(wrapped, kb - n_kb, kb)
        return kb, pl.multiple_of(kb * tk, tk), wrapped

    def scores(r, s_scr):
        _, k0, wrapped = block(r)
        k_blk = k_ref[0, pl.ds(k0, tk), :]
        if r == 0:
            s_scr[...] = lax.dot_general(k_blk, q2d, nt, preferred_element_type=F32)
        else:
            s_scr[...] = lax.dot_general(jnp.concatenate([k_blk, kaug_scr[...]], axis=1),
                                         q2_scr[jnp.where(wrapped, 0, 1)], nt,
                                         preferred_element_type=F32)

    def fold8(x, op):
        out = x[0:8]
        for g in range(1, x.shape[0] // 8):
            out = op(out, x[8 * g:8 * (g + 1)])
        return out

    def softmax(r, s_scr, p_scr, a_scr):
        _, k0, wrapped = block(r)
        if r == 0:
            beta = 0.0
        else:
            cf = slope * (qi * tq - k0).astype(F32)
            beta = jnp.where(wrapped, -cf, cf)
        mx = None
        for c in range(tk // kc):
            rows = slice(c * kc, (c + 1) * kc)
            blk = s_scr[rows, :]
            if r == 0:
                blk = blk - bias_scr[par, rows, :]
                s_scr[rows, :] = blk
            part = fold8(blk, jnp.maximum)
            mx = part if mx is None else jnp.maximum(mx, part)
        m_old = m_scr[...]
        m_new = jnp.maximum(m_old, jnp.max(mx, axis=0, keepdims=True) + beta)
        alpha = jnp.exp2(m_old - m_new)
        shift = m_new - beta
        psum = None
        for c in range(tk // kc):
            rows = slice(c * kc, (c + 1) * kc)
            p = jnp.exp2(s_scr[rows, :] - shift)
            p_scr[rows, :] = p.astype(BF16)
            part = fold8(p, jnp.add)
            psum = part if psum is None else psum + part
        l_scr[...] = alpha * l_scr[...] + psum
        m_scr[...] = m_new
        a_scr[...] = alpha

    def values(r, p_scr, a_scr):
        kb, _, _ = block(r)
        acc_scr[...] = a_scr[...] * acc_scr[...] + jnp.dot(
            vt_ref[0, kb], p_scr[...], preferred_element_type=F32)

    m_scr[...] = jnp.full_like(m_scr, NEG_BIG)
    l_scr[...] = jnp.zeros_like(l_scr)
    acc_scr[...] = jnp.zeros_like(acc_scr)
    s_bufs, p_bufs, a_bufs = (s0_scr, s1_scr), (p0_scr, p1_scr), (a0_scr, a1_scr)
    scores(0, s0_scr)
    for r in range(n_kb):
        cur, oth = r % 2, 1 - r % 2
        if r + 1 < n_kb:
            scores(r + 1, s_bufs[oth])
        softmax(r, s_bufs[cur], p_bufs[cur], a_bufs[cur])
        if r >= 1:
            values(r - 1, p_bufs[oth], a_bufs[oth])
    values(n_kb - 1, p_bufs[(n_kb - 1) % 2], a_bufs[(n_kb - 1) % 2])
    o2t = acc_scr[...] / jnp.sum(l_scr[...], axis=0, keepdims=True)
    lv = lam_ref[...]
    lam = (jnp.exp(jnp.sum(lv[0:1] * lv[1:2], axis=-1, keepdims=True))
           - jnp.exp(jnp.sum(lv[2:3] * lv[3:4], axis=-1, keepdims=True)) + lam_init)
    o = (o2t[:, :tq] - lam * o2t[:, tq:]).T
    o_ref[0] = (_rms(o, g_ref[...]) * (1.0 - lam_init)).astype(BF16)


def _df_attention(proj, vt, lam_vecs, g, lam_init, tq=512, tk=DF_TK, kc=128):
    b, s, _ = proj.shape
    tq = min(tq, s)
    assert tk % tq == 0 and s % tk == 0 and tk % kc == 0
    n_kb = s // tk
    qb, kb = 12, 16
    return pl.pallas_call(
        functools.partial(_df_kernel, tq=tq, tk=tk, kc=kc, s_len=s, lam_init=lam_init),
        grid=(b, DIFF_HEADS, s // tq),
        in_specs=[pl.BlockSpec((1, tq, LANES), lambda i, h, j: (i, j, qb + h)),
                  pl.BlockSpec((1, s, LANES), lambda i, h, j: (i, 0, kb + h)),
                  pl.BlockSpec((1, n_kb, LANES, tk), lambda i, h, j: (i, 0, h, 0)),
                  pl.BlockSpec((4, HEAD_DIM), lambda i, h, j: (0, 0)),
                  pl.BlockSpec((1, LANES), lambda i, h, j: (0, 0))],
        out_specs=pl.BlockSpec((1, tq, LANES), lambda i, h, j: (i, j, h)),
        out_shape=jax.ShapeDtypeStruct((b, s, DIFF_HEADS * LANES), BF16),
        scratch_shapes=[pltpu.VMEM((tk // tq, tk, 2 * tq), F32),
                        pltpu.VMEM((tk, LANES), BF16),
                        pltpu.VMEM((2, 2 * tq, 2 * LANES), BF16),
                        pltpu.VMEM((tk, 2 * tq), F32),
                        pltpu.VMEM((tk, 2 * tq), F32),
                        pltpu.VMEM((tk, 2 * tq), BF16),
                        pltpu.VMEM((tk, 2 * tq), BF16),
                        pltpu.VMEM((1, 2 * tq), F32),
                        pltpu.VMEM((1, 2 * tq), F32),
                        pltpu.VMEM((1, 2 * tq), F32),
                        pltpu.VMEM((8, 2 * tq), F32),
                        pltpu.VMEM((LANES, 2 * tq), F32)],
        compiler_params=_cparams(("parallel", "parallel", "arbitrary")),
        name="df_attn",
    )(proj, proj, vt, lam_vecs, g)


def _outproj_kernel(ona_ref, odf_ref, x_ref, wo_ref, g_ref, wr_ref, h1_ref, xn_ref, aff_ref):
    half = ona_ref.shape[-1]
    acc = jnp.dot(ona_ref[...], wo_ref[:half, :], preferred_element_type=F32)
    acc = acc + jnp.dot(odf_ref[...], wo_ref[half:, :], preferred_element_type=F32)
    h1 = x_ref[...] + acc
    h1_ref[...] = h1
    xn = _rms(h1, g_ref[...]).astype(BF16)
    xn_ref[...] = xn
    logits = jnp.dot(xn, wr_ref[...], preferred_element_type=F32)
    z = jnp.exp(logits - jnp.max(logits, axis=-1, keepdims=True))
    aff_ref[...] = z / jnp.sum(z, axis=-1, keepdims=True)


def _outproj(o_na, o_df, x2, wo, g, wr, tm=1024):
    n, d = x2.shape
    half = o_na.shape[-1]
    e = wr.shape[-1]
    return pl.pallas_call(
        _outproj_kernel,
        grid=(n // tm,),
        in_specs=[pl.BlockSpec((tm, half), lambda i: (i, 0)),
                  pl.BlockSpec((tm, half), lambda i: (i, 0)),
                  pl.BlockSpec((tm, d), lambda i: (i, 0)),
                  pl.BlockSpec((2 * half, d), lambda i: (0, 0)),
                  pl.BlockSpec((1, d), lambda i: (0, 0)),
                  pl.BlockSpec((d, e), lambda i: (0, 0))],
        out_specs=[pl.BlockSpec((tm, d), lambda i: (i, 0)),
                   pl.BlockSpec((tm, d), lambda i: (i, 0)),
                   pl.BlockSpec((tm, e), lambda i: (i, 0))],
        out_shape=[jax.ShapeDtypeStruct((n, d), F32),
                   jax.ShapeDtypeStruct((n, d), BF16),
                   jax.ShapeDtypeStruct((n, e), F32)],
        compiler_params=_cparams(("parallel",)),
        name="outproj",
    )(o_na, o_df, x2, wo, g, wr)


def _excl_prefix(mask, upper):
    e, s = mask.shape
    mb = jnp.where(mask, 1.0, 0.0).astype(BF16)
    carry = jnp.zeros((e, 1), F32)
    outs = []
    for c in range(s // LANES):
        xc = mb[:, c * LANES:(c + 1) * LANES]
        outs.append(jnp.dot(xc, upper, preferred_element_type=F32) + carry)
        carry = carry + jnp.sum(xc.astype(F32), axis=-1, keepdims=True)
    return jnp.concatenate(outs, axis=1)


def _route_kernel(aff_ref, slot_ref, gate_ref, pref_ref, *, cap):
    a = aff_ref[0]
    e = a.shape[0]

    def count_ge(t):
        return jnp.sum(jnp.where(a >= t, 1, 0), axis=-1, keepdims=True)

    def bit_body(i, bits):
        trial = bits | jnp.left_shift(jnp.int32(1), 30 - i)
        return jnp.where(count_ge(pltpu.bitcast(trial, F32)) >= cap, trial, bits)

    thr = pltpu.bitcast(lax.fori_loop(0, 31, bit_body, jnp.zeros((e, 1), jnp.int32)), F32)
    step0 = thr
    for j in range(1, 21):
        trial = thr + step0 * (2.0 ** -(24 + j))
        thr = jnp.where(count_ge(trial) >= cap, trial, thr)
    gt = a > thr
    eq = a == thr
    need = (cap - jnp.sum(jnp.where(gt, 1, 0), axis=-1, keepdims=True)).astype(F32)
    r = lax.broadcasted_iota(jnp.int32, (LANES, LANES), 0)
    c = lax.broadcasted_iota(jnp.int32, (LANES, LANES), 1)
    upper = jnp.where(r < c, 1.0, 0.0).astype(BF16)
    sel = gt | (eq & (_excl_prefix(eq, upper) < need))
    pref = _excl_prefix(sel, upper)
    pref_ref[0] = pref
    slot_ref[0] = jnp.where(sel, pref, -1.0)
    gate_ref[0] = jnp.where(sel, a, 0.0)


def _route(aff_t, cap):
    b, e, s = aff_t.shape
    spec = pl.BlockSpec((1, e, s), lambda i: (i, 0, 0))
    shp = jax.ShapeDtypeStruct((b, e, s), F32)
    return pl.pallas_call(
        functools.partial(_route_kernel, cap=cap),
        grid=(b,),
        in_specs=[spec],
        out_specs=[spec, spec, spec],
        out_shape=[shp, shp, shp],
        compiler_params=_cparams(("parallel",)),
        name="route",
    )(aff_t)


def _window_range(ts_ref, base, t):
    s0 = ts_ref[base + t]
    s1 = ts_ref[base + t + 1]
    w_lo = jnp.right_shift(s0, int(math.log2(MOE_WIN)))
    w_hi = jnp.where(s1 > s0, jnp.right_shift(s1 - 1, int(math.log2(MOE_WIN))) + 1, w_lo)
    return w_lo, w_hi


def _moe_ffn_kernel(ts_ref, xn_ref, slot_ref, gate_ref, wg_ref, wu_ref, wd_ref, y_ref, xs_acc, gs_acc,
                    *, n_tile, n_fchunk):
    b = pl.program_id(0)
    e = pl.program_id(1)
    base = (b * pl.num_programs(1) + e) * (n_tile + 1)
    xs_acc[...] = jnp.zeros_like(xs_acc)
    gs_acc[...] = jnp.zeros_like(gs_acc)
    sub = lax.broadcasted_iota(jnp.int32, (MOE_WIN, MOE_TILE), 0).astype(F32)
    cap = xs_acc.shape[0]
    last_win = cap // MOE_WIN - 1

    def add_rows(r0, first_slot, srow, grow, xt):
        match = srow - first_slot.astype(F32) == sub
        onehot = jnp.where(match, 1.0, 0.0).astype(BF16)
        xs_acc[pl.ds(r0, MOE_WIN), :] += jnp.dot(onehot, xt, preferred_element_type=F32)
        gs_acc[pl.ds(r0, MOE_WIN), :] += jnp.sum(jnp.where(match, grow, 0.0), axis=-1, keepdims=True)

    def compact(w, srow, grow, xt):
        add_rows(pl.multiple_of(jnp.minimum(w, last_win) * MOE_WIN, MOE_WIN), w * MOE_WIN, srow, grow, xt)

    n_max = ts_ref[base + 1] - ts_ref[base]
    for t in range(1, n_tile):
        n_max = jnp.maximum(n_max, ts_ref[base + t + 1] - ts_ref[base + t])
    single = n_max <= MOE_WIN - 7

    @pl.when(single)
    def _():
        for t in range(n_tile):
            s0 = ts_ref[base + t]
            start = jnp.minimum(jnp.left_shift(jnp.right_shift(s0, 3), 3), cap - MOE_WIN)
            add_rows(pl.multiple_of(start, 8), start, slot_ref[0, 0, t], gate_ref[0, 0, t],
                     xn_ref[0, t * MOE_TILE:(t + 1) * MOE_TILE, :])

    @pl.when(jnp.logical_not(single))
    def _():
        def tile_body(t, carry):
            w_lo, w_hi = _window_range(ts_ref, base, t)
            srow = slot_ref[0, 0, t]
            grow = gate_ref[0, 0, t]
            xt = xn_ref[0, pl.ds(pl.multiple_of(t * MOE_TILE, MOE_TILE), MOE_TILE), :]

            def win_body(w, c2):
                compact(w, srow, grow, xt)
                return c2

            lax.fori_loop(w_lo + 2, w_hi, win_body, 0)
            return carry

        lax.fori_loop(0, n_tile, tile_body, 0)

        for t in range(n_tile):
            w_lo = jnp.right_shift(ts_ref[base + t], int(math.log2(MOE_WIN)))
            srow = slot_ref[0, 0, t]
            grow = gate_ref[0, 0, t]
            xt = xn_ref[0, t * MOE_TILE:(t + 1) * MOE_TILE, :]
            compact(w_lo, srow, grow, xt)
            compact(w_lo + 1, srow, grow, xt)

    xs = xs_acc[...].astype(BF16)
    f = wg_ref.shape[-1]
    fc = f // n_fchunk
    y = None
    for j in range(n_fchunk):
        fs = slice(j * fc, (j + 1) * fc)
        gt = jnp.dot(xs, wg_ref[0, :, fs].astype(BF16), preferred_element_type=F32)
        up = jnp.dot(xs, wu_ref[0, :, fs].astype(BF16), preferred_element_type=F32)
        hid = (gt * jax.nn.sigmoid(gt) * up).astype(BF16)
        part = jnp.dot(hid, wd_ref[0, fs, :].astype(BF16), preferred_element_type=F32)
        y = part if y is None else y + part
    y_ref[0, 0] = (y * gs_acc[...]).astype(BF16)


def _moe_ffn(tstart, xn, slot_rows, gate_rows, wg, wu, wd, cap):
    b, s, d = xn.shape
    e, _, f = wg.shape
    n_tile = s // MOE_TILE
    row_spec = pl.BlockSpec((1, 1, n_tile, 1, MOE_TILE), lambda i, j, ts: (i, j, 0, 0, 0))
    grid_spec = pltpu.PrefetchScalarGridSpec(
        num_scalar_prefetch=1,
        grid=(b, e),
        in_specs=[pl.BlockSpec((1, s, d), lambda i, j, ts: (i, 0, 0)),
                  row_spec, row_spec,
                  pl.BlockSpec((1, d, f), lambda i, j, ts: (j, 0, 0)),
                  pl.BlockSpec((1, d, f), lambda i, j, ts: (j, 0, 0)),
                  pl.BlockSpec((1, f, d), lambda i, j, ts: (j, 0, 0))],
        out_specs=pl.BlockSpec((1, 1, cap, d), lambda i, j, ts: (i, j, 0, 0)),
        scratch_shapes=[pltpu.VMEM((cap, d), F32), pltpu.VMEM((cap, 1), F32)],
    )
    return pl.pallas_call(
        functools.partial(_moe_ffn_kernel, n_tile=n_tile, n_fchunk=max(1, f // 256)),
        grid_spec=grid_spec,
        out_shape=jax.ShapeDtypeStruct((b, e, cap, d), BF16),
        compiler_params=_cparams(("parallel", "arbitrary")),
        name="moe_ffn",
    )(tstart, xn, slot_rows, gate_rows, wg, wu, wd)


def _window_start(s0, cap, win):
    return jnp.minimum(jnp.left_shift(jnp.right_shift(s0, 4), 4), cap - win)


def _moe_comb_kernel(ts_ref, y_ref, slot_ref, rel_ref, o_ref, ycat_scr, *, n_tile, win):
    b = pl.program_id(0)
    t = pl.program_id(1)
    n_exp, cap = y_ref.shape[1], y_ref.shape[2]

    def first_slot(e):
        return ts_ref[(b * n_exp + e) * (n_tile + 1) + t]

    n_max = ts_ref[b * n_exp * (n_tile + 1) + t + 1] - first_slot(0)
    for e in range(1, n_exp):
        n_max = jnp.maximum(n_max, ts_ref[(b * n_exp + e) * (n_tile + 1) + t + 1] - first_slot(e))
    packed = n_max <= COMB_PACK - 15

    @pl.when(packed)
    def _():
        for e in range(n_exp):
            start = pl.multiple_of(_window_start(first_slot(e), cap, COMB_PACK), 16)
            ycat_scr[e * COMB_PACK:(e + 1) * COMB_PACK, :] = y_ref[0, e, pl.ds(start, COMB_PACK), :]
        width = n_exp * COMB_PACK
        shift = int(math.log2(COMB_PACK))
        lane_e = jnp.right_shift(lax.broadcasted_iota(jnp.int32, (n_exp, width), 1), shift)
        spread = jnp.where(lane_e == lax.broadcasted_iota(jnp.int32, (n_exp, width), 0), 1.0, 0.0)
        rel = jnp.dot(rel_ref[0], spread.astype(BF16), preferred_element_type=F32)
        lane_j = jnp.bitwise_and(lax.broadcasted_iota(jnp.int32, (COMB_TILE, width), 1), COMB_PACK - 1)
        onehot = jnp.where(rel == lane_j.astype(F32), 1.0, 0.0).astype(BF16)
        o_ref[0] = jnp.dot(onehot, ycat_scr[...], preferred_element_type=F32)

    @pl.when(jnp.logical_not(packed))
    def _():
        lane = lax.broadcasted_iota(jnp.int32, (COMB_TILE, win), 1).astype(F32)
        acc = None
        for e in range(n_exp):
            start = pl.multiple_of(_window_start(first_slot(e), cap, win), 16)
            scol = jnp.broadcast_to(slot_ref[0, 0, e], (LANES, COMB_TILE)).T - start.astype(F32)
            scol = jnp.tile(scol, (1, win // LANES))
            onehot = jnp.where(scol == lane, 1.0, 0.0).astype(BF16)
            part = jnp.dot(onehot, y_ref[0, e, pl.ds(start, win), :], preferred_element_type=F32)
            acc = part if acc is None else acc + part
        o_ref[0] = acc


def _moe_combine(tstart, y, slot_rows, rel, s):
    b, e, cap, d = y.shape
    n_tile = s // COMB_TILE
    win = min(cap, 2 * COMB_TILE)
    assert win == cap or win >= COMB_TILE + 15
    grid_spec = pltpu.PrefetchScalarGridSpec(
        num_scalar_prefetch=1,
        grid=(b, n_tile),
        in_specs=[pl.BlockSpec((1, e, cap, d), lambda i, j, ts: (i, 0, 0, 0)),
                  pl.BlockSpec((1, 1, e, 1, COMB_TILE), lambda i, j, ts: (i, j, 0, 0, 0)),
                  pl.BlockSpec((1, COMB_TILE, e), lambda i, j, ts: (i, j, 0))],
        out_specs=pl.BlockSpec((1, COMB_TILE, d), lambda i, j, ts: (i, j, 0)),
        scratch_shapes=[pltpu.VMEM((e * COMB_PACK, d), BF16)],
    )
    return pl.pallas_call(
        functools.partial(_moe_comb_kernel, n_tile=n_tile, win=win),
        grid_spec=grid_spec,
        out_shape=jax.ShapeDtypeStruct((b, s, d), F32),
        compiler_params=_cparams(("parallel", "arbitrary")),
        name="moe_comb",
    )(tstart, y, slot_rows, rel)


def _ple_kernel(h1_ref, moe_ref, p_ref, gp_ref, wg_ref, wp_ref, gf_ref, o_ref, *, last_layer):
    h2 = h1_ref[...] + moe_ref[...]
    hn = _rms(h2, gp_ref[...]).astype(BF16)
    gate = jax.nn.sigmoid(jnp.dot(hn, wg_ref[...], preferred_element_type=F32))
    proj = jnp.dot(p_ref[...].astype(BF16), wp_ref[...], preferred_element_type=F32)
    h3 = h2 + gate * proj
    o_ref[...] = _rms(h3, gf_ref[...]) if last_layer else h3


def _ple(h1, moe, p2, gp, wg, wp, gf, last_layer, tm=1024):
    n, d = h1.shape
    dp = p2.shape[-1]
    row = pl.BlockSpec((tm, d), lambda i: (i, 0))
    vec = pl.BlockSpec((1, d), lambda i: (0, 0))
    return pl.pallas_call(
        functools.partial(_ple_kernel, last_layer=last_layer),
        grid=(n // tm,),
        in_specs=[row, row, pl.BlockSpec((tm, dp), lambda i: (i, 0)), vec,
                  pl.BlockSpec((d, d), lambda i: (0, 0)), pl.BlockSpec((dp, d), lambda i: (0, 0)), vec],
        out_specs=row,
        out_shape=jax.ShapeDtypeStruct((n, d), F32),
        compiler_params=_cparams(("parallel",)),
        name="ple",
    )(h1, moe, p2, gp, wg, wp, gf)


def kernel(x, p, g_mix, w_in, na_rpb, g_na_out, lam_q1, lam_k1, lam_q2, lam_k2, g_diff_out, w_out,
           g_moe, w_router, w_gate, w_up, w_down, g_ple, w_ple_gate, w_ple_proj, g_final):
    b, s, d = x.shape
    depth = w_in.shape[0]
    n = b * s
    na_w = NA_HEADS * HEAD_DIM
    cap = EC_CAPACITY_FACTOR * s // N_EXPERTS
    n_tile = s // MOE_TILE
    col_scale = np.ones((w_in.shape[-1],), np.float32)
    col_scale[:na_w] = HEAD_DIM ** -0.5 * LOG2E
    col_scale[3 * na_w:3 * na_w + DIFF_HEADS * 2 * HEAD_DIM] = HEAD_DIM ** -0.5 * LOG2E

    h = x.reshape(n, d)
    for i in range(depth):
        lam_init = 0.8 - 0.6 * math.exp(-0.3 * i)
        w_s = (w_in[i] * col_scale).astype(BF16)
        n_v = DIFF_HEADS * 2 * HEAD_DIM
        proj, vt = _inproj(h, g_mix[i][None], w_s[:, :-n_v], w_s[:, -n_v:].T, s)
        proj = proj.reshape(b, s, -1)
        o_na = _na_attention(proj, _na_bias_table(na_rpb[i]), g_na_out[i].reshape(1, na_w))
        lam_vecs = jnp.stack([lam_q1[i], lam_k1[i], lam_q2[i], lam_k2[i]]).astype(F32)
        o_df = _df_attention(proj, vt, lam_vecs, g_diff_out[i][None], lam_init)
        h1, xn, aff = _outproj(o_na.reshape(n, -1), o_df.reshape(n, -1), h, w_out[i].astype(BF16),
                               g_moe[i][None], w_router[i].astype(BF16))
        aff_t = aff.reshape(b, s, N_EXPERTS).transpose(0, 2, 1)
        slot, gate, pref = _route(aff_t, cap)
        last = jnp.full((b, N_EXPERTS, 1), cap, F32)
        tstart = jnp.concatenate([pref[:, :, ::MOE_TILE], last], axis=-1).astype(jnp.int32).reshape(-1)
        cstart = jnp.concatenate([pref[:, :, ::COMB_TILE], last], axis=-1).astype(jnp.int32).reshape(-1)
        slot_rows = slot.reshape(b, N_EXPERTS, n_tile, 1, MOE_TILE)
        gate_rows = gate.reshape(b, N_EXPERTS, n_tile, 1, MOE_TILE)
        y = _moe_ffn(tstart, xn.reshape(b, s, d), slot_rows, gate_rows, w_gate[i], w_up[i], w_down[i], cap)
        comb_rows = slot.reshape(b, N_EXPERTS, s // COMB_TILE, 1, COMB_TILE).transpose(0, 2, 1, 3, 4)
        pack_start = _window_start(pref[:, :, ::COMB_TILE].astype(jnp.int32), cap, COMB_PACK)
        rel = slot - jnp.repeat(pack_start, COMB_TILE, axis=-1).astype(F32)
        rel = jnp.where(slot >= 0, rel, 255.0).transpose(0, 2, 1).astype(BF16)
        moe = _moe_combine(cstart, y, comb_rows, rel, s)
        h = _ple(h1, moe.reshape(n, d), p[i].reshape(n, -1), g_ple[i][None],
                 w_ple_gate[i].astype(BF16), w_ple_proj[i].astype(BF16), g_final[None],
                 last_layer=(i == depth - 1))
    return h.reshape(b, s, d)
```
